```python
import jax, jax.numpy as jnp
from jax import lax
import numpy as np

D_MODEL = 1024
BATCH = 8
SEQ = 2048
DEPTH = 4

CTX_LEN = 256
GRID_W = 64
RET_HEADS = 8
RET_DK = 64
RET_DV = 128
RET_QK = RET_HEADS * RET_DK
RET_V = RET_HEADS * RET_DV
RET_CHUNK = 128
ROPE_BASE = 10000.0
LRU_W = D_MODEL
LRU_BLOCKS = 16
LRU_BW = LRU_W // LRU_BLOCKS
LRU_C = 8.0
CONV_W = 4
CONV_LEFT = 2
FFN_HIDDEN = 2816
FFN_RES = 0.5
N_MOD = 9
EPS = 1e-6
PROJ_SIZES = (RET_QK, RET_QK, RET_V, RET_V, LRU_W, LRU_W, D_MODEL, D_MODEL)
PROJ_W = 2 * RET_QK + 2 * RET_V + 2 * LRU_W + 2 * D_MODEL

kernel_name = "hybrid_retention_rglru_prefix_dit"


def rms_norm(x, g):
    xf = x.astype(jnp.float32)
    y = xf * lax.rsqrt(jnp.mean(xf * xf, axis=-1, keepdims=True) + EPS)
    return (y * g.astype(jnp.float32)).astype(x.dtype)


def ada_norm(x, g, shift, scale):
    return rms_norm(x, g) * (1 + scale) + shift


def ffn_sublayer(x, mod, g, w_gu, w_down):
    shift, scale, gate = mod
    h = ada_norm(x, g, shift, scale)
    u, v = jnp.split(h @ w_gu, 2, axis=-1)
    return x + FFN_RES * gate * ((jax.nn.silu(u) * v) @ w_down)


def split_proj(z):
    out, start = [], 0
    for size in PROJ_SIZES:
        out.append(z[..., start:start + size])
        start += size
    return out


def grid_rotary(rows):
    row = jnp.repeat(jnp.arange(rows, dtype=jnp.float32), GRID_W)
    col = jnp.tile(jnp.arange(GRID_W, dtype=jnp.float32), rows)
    n_f = RET_DK // 4
    inv = ROPE_BASE ** (-jnp.arange(n_f, dtype=jnp.float32) / n_f)
    ang = jnp.concatenate([row[:, None] * inv, col[:, None] * inv], axis=-1)
    return jnp.cos(ang), jnp.sin(ang)


def apply_rotary(a, cos, sin):
    a1, a2 = jnp.split(a, 2, axis=-1)
    cs, sn = cos[None, :, None, :], sin[None, :, None, :]
    return jnp.concatenate([a1 * cs - a2 * sn, a1 * sn + a2 * cs], axis=-1)


def retention_heads(z, cos, sin):
    bsz, t = z[0].shape[:2]
    q = z[0].reshape(bsz, t, RET_HEADS, RET_DK).astype(jnp.float32)
    k = z[1].reshape(bsz, t, RET_HEADS, RET_DK).astype(jnp.float32)
    v = z[2].reshape(bsz, t, RET_HEADS, RET_DV).astype(jnp.float32)
    if cos is not None:
        q = apply_rotary(q, cos, sin)
        k = apply_rotary(k, cos, sin)
    return q, k * (RET_DK ** -0.5), v


def retention_chunkwise(q, k, v, log_g, s0, include_diag):
    bsz, t, nh, _ = q.shape
    n = t // RET_CHUNK

    def to_chunks(a):
        return a.reshape(bsz, n, RET_CHUNK, nh, a.shape[-1]).transpose(1, 0, 3, 2, 4)

    idx = jnp.arange(RET_CHUNK, dtype=jnp.float32)
    rel = idx[:, None] - idx[None, :]
    mask = (rel >= 0) if include_diag else (rel > 0)
    lg = log_g[:, None, None]
    d_intra = jnp.where(mask[None], jnp.exp(lg * jnp.maximum(rel, 0.0)[None]), 0.0)
    q_dec = jnp.exp(log_g[:, None] * (idx + 1.0))[None, :, :, None]
    k_dec = jnp.exp(log_g[:, None] * (RET_CHUNK - 1.0 - idx))[None, :, :, None]
    c_dec = jnp.exp(log_g * RET_CHUNK)[None, :, None, None]

    def step(s, blk):
        qc, kc, vc = blk
        scores = jnp.einsum('bhid,bhjd->bhij', qc, kc) * d_intra
        o = (jnp.einsum('bhij,bhjv->bhiv', scores, vc)
             + jnp.einsum('bhid,bhdv->bhiv', qc * q_dec, s))
        s = s * c_dec + jnp.einsum('bhjd,bhjv->bhdv', kc * k_dec, vc)
        return s, o

    s, o = lax.scan(step, s0, (to_chunks(q), to_chunks(k), to_chunks(v)))
    o = o.transpose(1, 0, 3, 2, 4).reshape(bsz, t, nh, v.shape[-1])
    return o, s


def head_norm(o):
    mu = jnp.mean(o, axis=-1, keepdims=True)
    var = jnp.mean(jnp.square(o - mu), axis=-1, keepdims=True)
    return (o - mu) * lax.rsqrt(var + EPS)


def short_conv(u, w, b):
    t = u.shape[1]
    up = jnp.pad(u, ((0, 0), (CONV_LEFT, CONV_W - 1 - CONV_LEFT), (0, 0)))
    out = up[:, 0:t] * w[0]
    for j in range(1, CONV_W):
        out = out + up[:, j:j + t] * w[j]
    return out + b


def _lin_comb(l, r):
    return (l[0] * r[0], r[0] * l[1] + r[1])


def rg_lru_dir(u, wg, bg, lam, h0, reverse):
    if reverse:
        u = u[:, ::-1]
    bsz, t, w = u.shape
    g = jnp.einsum('btnk,gnkj->gbtnj', u.reshape(bsz, t, LRU_BLOCKS, LRU_BW),
                   wg.astype(jnp.float32)).reshape(2, bsz, t, w)
    g = g + bg.astype(jnp.float32)[:, None, None, :]
    r = jax.nn.sigmoid(g[0])
    i = jax.nn.sigmoid(g[1])
    log_a = -LRU_C * r * jax.nn.softplus(-lam.astype(jnp.float32))
    a = jnp.exp(log_a)
    b = jnp.sqrt(-jnp.expm1(2.0 * log_a)) * (i * u)
    a_cum, b_cum = lax.associative_scan(_lin_comb, (a, b), axis=1)
    h = a_cum * h0[:, None, :] + b_cum
    h_last = h[:, -1]
    if reverse:
        h = h[:, ::-1]
    return h, h_last


def merge_branches(z, o_ret, h_lru, w_ret_o, w_lru_o, w_out):
    g_ret, g_lru, gate_a, gate_b = z[3], z[5], z[6], z[7]
    bsz, t = o_ret.shape[:2]
    dt = g_ret.dtype
    o = head_norm(o_ret).reshape(bsz, t, RET_V).astype(dt)
    y_a = (o * jax.nn.silu(g_ret)) @ w_ret_o
    y_b = (h_lru.astype(dt) * jax.nn.gelu(g_lru)) @ w_lru_o
    return (jax.nn.sigmoid(gate_a) * y_a + jax.nn.sigmoid(gate_b) * y_b) @ w_out


def token_mixer(hc, hx, cos, sin, w_in, ret_logit, w_ret_o, conv_w, conv_b,
                gate_w, gate_b, lam, w_lru_o, w_out, with_ctx_out):
    zc = split_proj(hc @ w_in)
    zx = split_proj(hx @ w_in)
    bsz = hc.shape[0]
    log_g = jax.nn.log_sigmoid(ret_logit.astype(jnp.float32))

    qc, kc, vc = retention_heads(zc, None, None)
    qx, kx, vx = retention_heads(zx, cos, sin)
    s_init = jnp.zeros((bsz, RET_HEADS, RET_DK, RET_DV), jnp.float32)
    oc_f, s_f = retention_chunkwise(qc, kc, vc, log_g[0], s_init, True)
    oc_b, s_b = retention_chunkwise(qc[:, ::-1], kc[:, ::-1], vc[:, ::-1], log_g[1], s_init, False)
    ox_f, _ = retention_chunkwise(qx, kx, vx, log_g[0], s_f, True)
    ox_b, _ = retention_chunkwise(qx[:, ::-1], kx[:, ::-1], vx[:, ::-1], log_g[1], s_b, False)
    ox = ox_f + ox_b[:, ::-1]

    uc = short_conv(zc[4], conv_w, conv_b).astype(jnp.float32)
    ux = short_conv(zx[4], conv_w, conv_b).astype(jnp.float32)
    h0 = jnp.zeros((bsz, LRU_W), jnp.float32)
    hc_f, st_f = rg_lru_dir(uc, gate_w[0], gate_b[0], lam[0], h0, False)
    hc_b, st_b = rg_lru_dir(uc, gate_w[1], gate_b[1], lam[1], h0, True)
    hx_f, _ = rg_lru_dir(ux, gate_w[0], gate_b[0], lam[0], st_f, False)
    hx_b, _ = rg_lru_dir(ux, gate_w[1], gate_b[1], lam[1], st_b, True)

    y_x = merge_branches(zx, ox, hx_f + hx_b, w_ret_o, w_lru_o, w_out)
    y_c = None
    if with_ctx_out:
        y_c = merge_branches(zc, oc_f + oc_b[:, ::-1], hc_f + hc_b, w_ret_o, w_lru_o, w_out)
    return y_c, y_x


def setup_inputs(seed: int = 0) -> dict:
    key = jax.random.key(seed)
    ks = jax.random.split(key, 26)
    f32 = jnp.float32
    L, D, F = DEPTH, D_MODEL, FFN_HIDDEN

    def nrm(k, shape, fan_in):
        return jax.random.normal(k, shape, f32) * (fan_in ** -0.5)

    def small(k, shape, s=0.02):
        return s * jax.random.normal(k, shape, f32)

    gamma = 1.0 - 2.0 ** (-5.0 - jnp.arange(RET_HEADS, dtype=f32))
    ret_decay_logit = jnp.log(gamma / (1.0 - gamma)) + small(ks[12], (L, 2, RET_HEADS), 0.05)
    u = jax.random.uniform(ks[18], (L, 2, LRU_W), f32, 0.9, 0.999)
    a = u ** (1.0 / LRU_C)
    lru_lambda = jnp.log(a) - jnp.log1p(-a)

    return {
        "x": jax.random.normal(ks[0], (BATCH, SEQ, D), f32),
        "c": jax.random.normal(ks[1], (BATCH, D), f32),
        "ctx": jax.random.normal(ks[2], (BATCH, CTX_LEN, D), f32),
        "c_ctx": jax.random.normal(ks[3], (D,), f32),
        "w_mod": nrm(ks[4], (L, D, N_MOD * D), D),
        "b_mod": small(ks[5], (L, N_MOD * D)),
        "norm_g": 1.0 + small(ks[6], (L, 3, D)),
        "ffn1_w_gu": nrm(ks[7], (L, D, 2 * F), D),
        "ffn1_w_down": nrm(ks[8], (L, F, D), F),
        "ffn2_w_gu": nrm(ks[9], (L, D, 2 * F), D),
        "ffn2_w_down": nrm(ks[10], (L, F, D), F),
        "w_in": nrm(ks[11], (L, D, PROJ_W), D),
        "ret_decay_logit": ret_decay_logit,
        "w_ret_o": nrm(ks[13], (L, RET_V, D), RET_V),
        "lru_conv_w": nrm(ks[14], (L, CONV_W, LRU_W), CONV_W),
        "lru_conv_b": small(ks[15], (L, LRU_W)),
        "lru_gate_w": nrm(ks[16], (L, 2, 2, LRU_BLOCKS, LRU_BW, LRU_BW), LRU_BW),
        "lru_gate_b": small(ks[17], (L, 2, 2, LRU_W)),
        "lru_lambda": lru_lambda,
        "w_lru_o": nrm(ks[19], (L, LRU_W, D), LRU_W),
        "w_out": nrm(ks[20], (L, D, D), D),
        "final_g": 1.0 + small(ks[21], (D,)),
    }


def reference(x, c, ctx, c_ctx, w_mod, b_mod, norm_g, ffn1_w_gu, ffn1_w_down,
              ffn2_w_gu, ffn2_w_down, w_in, ret_decay_logit, w_ret_o, lru_conv_w,
              lru_conv_b, lru_gate_w, lru_gate_b, lru_lambda, w_lru_o, w_out, final_g):
    n_lat = x.shape[1]
    rows = n_lat // GRID_W
    cos, sin = grid_rotary(rows)
    for l in range(DEPTH):
        last = l == DEPTH - 1
        m_x = jnp.split((jax.nn.silu(c) @ w_mod[l] + b_mod[l])[:, None, :], N_MOD, axis=-1)
        m_c = jnp.split(jax.nn.silu(c_ctx) @ w_mod[l] + b_mod[l], N_MOD, axis=-1)

        ctx = ffn_sublayer(ctx, m_c[0:3], norm_g[l, 0], ffn1_w_gu[l], ffn1_w_down[l])
        x = ffn_sublayer(x, m_x[0:3], norm_g[l, 0], ffn1_w_gu[l], ffn1_w_down[l])

        hc = ada_norm(ctx, norm_g[l, 1], m_c[3], m_c[4])
        hx = ada_norm(x, norm_g[l, 1], m_x[3], m_x[4])
        y_c, y_x = token_mixer(hc, hx, cos, sin, w_in[l], ret_decay_logit[l], w_ret_o[l],
                               lru_conv_w[l], lru_conv_b[l], lru_gate_w[l], lru_gate_b[l],
                               lru_lambda[l], w_lru_o[l], w_out[l], not last)
        x = x + m_x[5] * y_x

        x = ffn_sublayer(x, m_x[6:9], norm_g[l, 2], ffn2_w_gu[l], ffn2_w_down[l])
        if not last:
            ctx = ctx + m_c[5] * y_c
            ctx = ffn_sublayer(ctx, m_c[6:9], norm_g[l, 2], ffn2_w_gu[l], ffn2_w_down[l])
    return rms_norm(x, final_g)
```

```python
import functools

import jax
import jax.numpy as jnp
from jax import lax
from jax.experimental import pallas as pl
from jax.experimental.pallas import tpu as pltpu

F32 = jnp.float32
BF16 = jnp.bfloat16

D_MODEL = 1024
BATCH = 8
RET_HEADS = 8
RET_DK = 64
RET_DV = 128
RET_CHUNK = 128
HEAD_GROUP = 4
GRID_W = 64
ROPE_BASE = 10000.0
LRU_BLOCKS = 16
LRU_BW = D_MODEL // LRU_BLOCKS
LRU_C = 8.0
LRU_CW = 256
LRU_TB = 128
CONV_W = 4
FFN_HIDDEN = 2816
FFN_TF = 256
FFN_RES = 0.5
N_MOD = 9
EPS = 1e-6
PROJ_CHUNK = 512
N_PROJ_CHUNKS = 14
ROW_TILE = 512
VMEM_LIMIT = 52 * 1024 * 1024


def _cparams(*sem):
    return pltpu.CompilerParams(dimension_semantics=sem, vmem_limit_bytes=VMEM_LIMIT)


def _resident(shape):
    return pl.BlockSpec(shape, lambda *_: (0,) * len(shape), pipeline_mode=pl.Buffered(1))


def _sigmoid(x):
    return 1.0 / (1.0 + jnp.exp(-x))


def _ada_norm(x, g, shift, scale):
    rows = x.shape[0]
    ms = jnp.mean(x * x, axis=-1, keepdims=True)
    y = (x * lax.rsqrt(ms + EPS)) * g
    y3 = y.reshape(rows // BATCH, BATCH, D_MODEL)
    h = y3 * (1.0 + scale)[None] + shift[None]
    return h.reshape(rows, D_MODEL)


def _mod_kernel(c_ref, w_ref, b_ref, o_ref):
    c = c_ref[...]
    s = (c * _sigmoid(c)).astype(BF16)
    o_ref[...] = jnp.dot(s, w_ref[...].astype(BF16), preferred_element_type=F32) + b_ref[...]


def _mod_tables(c, c_ctx, w_mod, b_mod):
    depth = w_mod.shape[0]
    cc = jnp.zeros((2 * BATCH, D_MODEL), F32).at[:BATCH].set(c).at[BATCH].set(c_ctx)
    out = pl.pallas_call(
        _mod_kernel,
        grid=(depth, N_MOD),
        in_specs=[
            pl.BlockSpec((2 * BATCH, D_MODEL), lambda l, j: (0, 0)),
            pl.BlockSpec((None, D_MODEL, D_MODEL), lambda l, j: (l, 0, j)),
            pl.BlockSpec((None, 1, D_MODEL), lambda l, j: (l, 0, j)),
        ],
        out_specs=pl.BlockSpec((None, 2 * BATCH, D_MODEL), lambda l, j: (l, 0, j)),
        out_shape=jax.ShapeDtypeStruct((depth, 2 * BATCH, N_MOD * D_MODEL), F32),
        compiler_params=_cparams("parallel", "parallel"),
        name="adaln_mod",
    )(cc, w_mod, b_mod.reshape(depth, 1, N_MOD * D_MODEL))
    out = out.reshape(depth, 2 * BATCH, N_MOD, D_MODEL)
    lat = out[:, :BATCH].transpose(0, 2, 1, 3)
    ctx = jnp.broadcast_to(out[:, BATCH][:, :, None, :], lat.shape)
    return jnp.stack([ctx, lat], axis=1)


def _ffn_kernel(x_ref, mod_ref, g_ref, wg_ref, wu_ref, wd_ref, o_ref, h_scr, acc_scr):
    rows = x_ref.shape[0]
    x = x_ref[...]
    h_scr[...] = _ada_norm(x, g_ref[...], mod_ref[0], mod_ref[1]).astype(BF16)
    acc_scr[...] = jnp.zeros_like(acc_scr)

    def body(k, carry):
        hb = h_scr[...]
        u = jnp.dot(hb, wg_ref[k], preferred_element_type=F32)
        v = jnp.dot(hb, wu_ref[k], preferred_element_type=F32)
        a = ((u * _sigmoid(u)) * v).astype(BF16)
        acc_scr[...] += jnp.dot(a, wd_ref[k], preferred_element_type=F32)
        return carry

    lax.fori_loop(0, wg_ref.shape[0], body, 0)
    y3 = acc_scr[...].reshape(rows // BATCH, BATCH, D_MODEL)
    x3 = x.reshape(rows // BATCH, BATCH, D_MODEL)
    o_ref[...] = (x3 + (FFN_RES * mod_ref[2])[None] * y3).reshape(rows, D_MODEL)


def _ffn_sublayer(xs, tab, sub, g, wg, wu, wd, ctx_rows, row_start):
    n = xs.shape[0]
    tm = ROW_TILE
    t0 = row_start // tm
    nf = wg.shape[0]
    return pl.pallas_call(
        _ffn_kernel,
        grid=(n // tm - t0,),
        in_specs=[
            pl.BlockSpec((tm, D_MODEL), lambda i: (i + t0, 0)),
            pl.BlockSpec((None, 3, BATCH, D_MODEL),
                         lambda i: (((i + t0) * tm >= ctx_rows).astype(jnp.int32), sub // 3, 0, 0)),
            _resident((1, D_MODEL)),
            _resident((nf, D_MODEL, FFN_TF)),
            _resident((nf, D_MODEL, FFN_TF)),
            _resident((nf, FFN_TF, D_MODEL)),
        ],
        out_specs=pl.BlockSpec((tm, D_MODEL), lambda i: (i + t0, 0)),
        out_shape=jax.ShapeDtypeStruct((n, D_MODEL), F32),
        scratch_shapes=[pltpu.VMEM((tm, D_MODEL), BF16), pltpu.VMEM((tm, D_MODEL), F32)],
        compiler_params=_cparams("parallel"),
        name="ffn_sublayer",
    )(xs, tab, g, wg, wu, wd)


def _swap_halves(a):
    w = a.shape[1]
    lane = lax.broadcasted_iota(jnp.int32, a.shape, 1)
    first_half = (lane % RET_DK) < (RET_DK // 2)
    return jnp.where(first_half, pltpu.roll(a, w - RET_DK // 2, 1), pltpu.roll(a, RET_DK // 2, 1))


def _inproj_kernel(x_ref, mod_ref, g_ref, cos_ref, sin_ref, w_ref, q_ref, k_ref, z_ref, h_scr):
    h_scr[...] = _ada_norm(x_ref[...], g_ref[...], mod_ref[0], mod_ref[1]).astype(BF16)
    reps = PROJ_CHUNK // cos_ref.shape[1]
    cos = jnp.tile(cos_ref[...], (1, reps))
    sin = jnp.tile(sin_ref[...], (1, reps))

    def rotate(a):
        return a * cos + _swap_halves(a) * sin

    q = jnp.dot(h_scr[...], w_ref[0], preferred_element_type=F32)
    q_ref[...] = rotate(q).astype(BF16)
    k = jnp.dot(h_scr[...], w_ref[1], preferred_element_type=F32)
    k_ref[...] = (rotate(k) * (RET_DK ** -0.5)).astype(BF16)

    def body(c, carry):
        z_ref[c] = jnp.dot(h_scr[...], w_ref[c + 2], preferred_element_type=F32).astype(BF16)
        return carry

    lax.fori_loop(0, N_PROJ_CHUNKS - 2, body, 0)


def _in_projection(xs, tab, g, cos_t, sin_t, w_in, ctx_rows):
    n = xs.shape[0]
    tm = ROW_TILE
    nz = N_PROJ_CHUNKS - 2
    return pl.pallas_call(
        _inproj_kernel,
        grid=(n // tm,),
        in_specs=[
            pl.BlockSpec((tm, D_MODEL), lambda i: (i, 0)),
            pl.BlockSpec((None, 3, BATCH, D_MODEL),
                         lambda i: ((i * tm >= ctx_rows).astype(jnp.int32), 1, 0, 0)),
            _resident((1, D_MODEL)),
            pl.BlockSpec((tm, 128), lambda i: (i, 0)),
            pl.BlockSpec((tm, 128), lambda i: (i, 0)),
            _resident((N_PROJ_CHUNKS, D_MODEL, PROJ_CHUNK)),
        ],
        out_specs=[
            pl.BlockSpec((tm, PROJ_CHUNK), lambda i: (i, 0)),
            pl.BlockSpec((tm, PROJ_CHUNK), lambda i: (i, 0)),
            pl.BlockSpec((nz, tm, PROJ_CHUNK), lambda i: (0, i, 0)),
        ],
        out_shape=[
            jax.ShapeDtypeStruct((n, PROJ_CHUNK), BF16),
            jax.ShapeDtypeStruct((n, PROJ_CHUNK), BF16),
            jax.ShapeDtypeStruct((nz, n, PROJ_CHUNK), BF16),
        ],
        scratch_shapes=[pltpu.VMEM((tm, D_MODEL), BF16)],
        compiler_params=_cparams("parallel"),
        name="in_projection",
    )(xs, tab, g, cos_t, sin_t, w_in)


def _log_sigmoid(x):
    return jnp.minimum(x, 0.0) - jnp.log1p(jnp.exp(-jnp.abs(x)))


def _dot_t0(a, b):
    return lax.dot_general(a, b, (((0,), (0,)), ((), ())), preferred_element_type=F32)


def _dot_t1(a, b):
    return lax.dot_general(a, b, (((1,), (1,)), ((), ())), preferred_element_type=F32)


def _retention_kernel(lg_ref, q_ref, k_ref, v_ref, g_ref, o_ref,
                      dmat, qdec, kdec, cdec, sb_scr, sf_scr, *, n_ctx_chunks):
    c_len = RET_CHUNK
    n_chunks = q_ref.shape[0] // c_len
    hg = pl.program_id(1)
    ri = lax.broadcasted_iota(jnp.int32, (c_len, c_len), 0)
    ci = lax.broadcasted_iota(jnp.int32, (c_len, c_len), 1)
    rel = (ri - ci).astype(F32)
    pos = lax.broadcasted_iota(jnp.int32, (c_len, RET_DK), 0).astype(F32)

    for hh in range(HEAD_GROUP):
        lgf = _log_sigmoid(lg_ref[0, hg * HEAD_GROUP + hh])
        lgb = _log_sigmoid(lg_ref[1, hg * HEAD_GROUP + hh])
        dmat[hh] = jnp.where(rel >= 0.0, jnp.exp(lgf * jnp.maximum(rel, 0.0)),
                             jnp.exp(lgb * jnp.maximum(-rel, 0.0)))
        lgf_k = lgf[:, :RET_DK]
        lgb_k = lgb[:, :RET_DK]
        qdec[0, hh] = jnp.exp(lgf_k * (pos + 1.0))
        qdec[1, hh] = jnp.exp(lgb_k * (c_len - pos))
        kdec[0, hh] = jnp.exp(lgf_k * (c_len - 1.0 - pos))
        kdec[1, hh] = jnp.exp(lgb_k * pos)
        cdec[0, hh] = jnp.exp(lgf * c_len)
        cdec[1, hh] = jnp.exp(lgb * c_len)

    def head(a, hh, width):
        return a[:, hh * width:(hh + 1) * width]

    sf_scr[...] = jnp.zeros_like(sf_scr)

    def bwd_body(i, carry):
        c = jnp.where(i < n_ctx_chunks, n_ctx_chunks - 1 - i, n_chunks - 1 + n_ctx_chunks - i)
        rows = pl.ds(pl.multiple_of(c * c_len, c_len), c_len)
        kc = k_ref[rows, :].astype(F32)
        vc = v_ref[rows, :]
        for hh in range(HEAD_GROUP):
            s = sf_scr[hh]
            sb_scr[c, hh] = s
            kd = (head(kc, hh, RET_DK) * kdec[1, hh]).astype(BF16)
            sf_scr[hh] = s * cdec[1, hh] + _dot_t0(kd, head(vc, hh, RET_DV))
        return carry

    lax.fori_loop(0, n_chunks, bwd_body, 0)

    sf_scr[...] = jnp.zeros_like(sf_scr)

    def fwd_body(c, carry):
        rows = pl.ds(pl.multiple_of(c * c_len, c_len), c_len)
        qb = q_ref[rows, :]
        kb = k_ref[rows, :]
        qc = qb.astype(F32)
        kc = kb.astype(F32)
        vc = v_ref[rows, :]
        gc = g_ref[rows, :].astype(F32)
        for hh in range(HEAD_GROUP):
            qh = head(qc, hh, RET_DK)
            vh = head(vc, hh, RET_DV)
            scores = _dot_t1(head(qb, hh, RET_DK), head(kb, hh, RET_DK)) * dmat[hh]
            o = jnp.dot(scores.astype(BF16), vh, preferred_element_type=F32)
            sf = sf_scr[hh]
            o += jnp.dot((qh * qdec[0, hh]).astype(BF16), sf.astype(BF16), preferred_element_type=F32)
            o += jnp.dot((qh * qdec[1, hh]).astype(BF16), sb_scr[c, hh].astype(BF16),
                         preferred_element_type=F32)
            kd = (head(kc, hh, RET_DK) * kdec[0, hh]).astype(BF16)
            sf_scr[hh] = sf * cdec[0, hh] + _dot_t0(kd, vh)
            mu = jnp.mean(o, axis=-1, keepdims=True)
            d = o - mu
            var = jnp.mean(d * d, axis=-1, keepdims=True)
            gh = head(gc, hh, RET_DV)
            o_ref[rows, hh * RET_DV:(hh + 1) * RET_DV] = (
                (d * lax.rsqrt(var + EPS)) * (gh * _sigmoid(gh))).astype(BF16)
        return carry

    lax.fori_loop(0, n_chunks, fwd_body, 0)


def _retention(q, k, z, logit_rows, t_all, n_ctx_chunks):
    n_groups = RET_HEADS // HEAD_GROUP
    qw = HEAD_GROUP * RET_DK
    vw = HEAD_GROUP * RET_DV
    nz = z.shape[0]
    q2 = q.reshape(t_all, BATCH * RET_HEADS * RET_DK)
    k2 = k.reshape(t_all, BATCH * RET_HEADS * RET_DK)
    z2 = z.reshape(nz, t_all, BATCH * PROJ_CHUNK)
    n_chunks = t_all // RET_CHUNK
    out = pl.pallas_call(
        functools.partial(_retention_kernel, n_ctx_chunks=n_ctx_chunks),
        grid=(BATCH, n_groups),
        in_specs=[
            _resident((2, RET_HEADS, 1, 128)),
            pl.BlockSpec((t_all, qw), lambda b, h: (0, b * n_groups + h)),
            pl.BlockSpec((t_all, qw), lambda b, h: (0, b * n_groups + h)),
            pl.BlockSpec((None, t_all, vw), lambda b, h: (h, 0, b)),
            pl.BlockSpec((None, t_all, vw), lambda b, h: (2 + h, 0, b)),
        ],
        out_specs=pl.BlockSpec((t_all, vw), lambda b, h: (0, b * n_groups + h)),
        out_shape=jax.ShapeDtypeStruct((t_all, BATCH * RET_HEADS * RET_DV), BF16),
        scratch_shapes=[
            pltpu.VMEM((HEAD_GROUP, RET_CHUNK, RET_CHUNK), F32),
            pltpu.VMEM((2, HEAD_GROUP, RET_CHUNK, RET_DK), F32),
            pltpu.VMEM((2, HEAD_GROUP, RET_CHUNK, RET_DK), F32),
            pltpu.VMEM((2, HEAD_GROUP, 1, 128), F32),
            pltpu.VMEM((n_chunks, HEAD_GROUP, RET_DK, RET_DV), F32),
            pltpu.VMEM((HEAD_GROUP, RET_DK, RET_DV), F32),
        ],
        compiler_params=_cparams("parallel", "parallel"),
        name="retention",
    )(logit_rows, q2, k2, z2, z2)
    return out.reshape(t_all * BATCH, RET_HEADS * RET_DV)


def _lru_kernel(xf_ref, xfp_ref, xfn_ref, xb_ref, xbp_ref, xbn_ref,
                cw_ref, cb_ref, gw_ref, gb_ref, lam_ref, hf_ref, hb_ref,
                a_scr, b_scr, h_scr, *, n_ctx_blocks, n_blocks):
    j = pl.program_id(1)
    jb = jnp.where(j < n_ctx_blocks, n_ctx_blocks - 1 - j, n_blocks - 1 + n_ctx_blocks - j)
    rows = xf_ref.shape[0]
    halo = xfp_ref.shape[0]
    cw = cw_ref[...]
    cb = cb_ref[...]

    def prepare(d, blk, x_ref, xp_ref, xn_ref):
        is_first = jnp.logical_or(blk == 0, blk == n_ctx_blocks)
        is_last = jnp.logical_or(blk == n_ctx_blocks - 1, blk == n_blocks - 1)
        prev = jnp.where(is_first, 0.0, xp_ref[...].astype(F32))
        nxt = jnp.where(is_last, 0.0, xn_ref[...].astype(F32))
        xe = jnp.concatenate([prev, x_ref[...].astype(F32), nxt], axis=0)
        u = cb
        for tap in range(CONV_W):
            off = halo - (2 - tap) * BATCH
            u = u + xe[off:off + rows] * cw[tap:tap + 1]
        g = jnp.dot(u.astype(BF16), gw_ref[d], preferred_element_type=F32) + gb_ref[d]
        r = _sigmoid(g[:, :LRU_CW])
        i = _sigmoid(g[:, LRU_CW:])
        lam = lam_ref[d]
        softplus = jnp.maximum(-lam, 0.0) + jnp.log1p(jnp.exp(-jnp.abs(lam)))
        log_a = (-LRU_C * r) * softplus
        th = jnp.tanh(log_a)
        one_minus_a2 = (-2.0 * th) / (1.0 - th)
        a_scr[d] = jnp.exp(log_a)
        b_scr[d] = jnp.sqrt(one_minus_a2) * (i * u)

    prepare(0, j, xf_ref, xfp_ref, xfn_ref)
    prepare(1, jb, xb_ref, xbp_ref, xbn_ref)

    @pl.when(j == 0)
    def _():
        h_scr[...] = jnp.zeros_like(h_scr)

    n_steps = rows // BATCH

    def step(s, carry):
        hf, hb = carry
        rf = pl.ds(pl.multiple_of(s * BATCH, BATCH), BATCH)
        hf = a_scr[0, rf, :] * hf + b_scr[0, rf, :]
        hf_ref[rf, :] = hf
        rb = pl.ds(pl.multiple_of((n_steps - 1 - s) * BATCH, BATCH), BATCH)
        hb = a_scr[1, rb, :] * hb + b_scr[1, rb, :]
        hb_ref[rb, :] = hb
        return hf, hb

    hf, hb = lax.fori_loop(0, n_steps, step, (h_scr[0], h_scr[1]), unroll=8)
    h_scr[0] = hf
    h_scr[1] = hb


def _rg_lru(z, conv_w, conv_b, gate_w, gate_b, lam, n_rows, n_ctx_blocks):
    rows = LRU_TB * BATCH
    n_blocks = n_rows // rows
    n_cc = D_MODEL // LRU_CW
    per_chunk = PROJ_CHUNK // LRU_CW
    halo = 16
    hb_per_block = rows // halo
    n_halo = n_rows // halo

    def bwd_block(j):
        return jnp.where(j < n_ctx_blocks, n_ctx_blocks - 1 - j, n_blocks - 1 + n_ctx_blocks - j)

    def cur(f):
        return pl.BlockSpec((None, rows, LRU_CW), lambda c, j: (4 + c // per_chunk, f(j), c % per_chunk))

    def prev(f):
        return pl.BlockSpec((None, halo, LRU_CW),
                            lambda c, j: (4 + c // per_chunk, jnp.maximum(f(j) * hb_per_block - 1, 0),
                                          c % per_chunk))

    def nxt(f):
        return pl.BlockSpec((None, halo, LRU_CW),
                            lambda c, j: (4 + c // per_chunk,
                                          jnp.minimum((f(j) + 1) * hb_per_block, n_halo - 1),
                                          c % per_chunk))

    ident = lambda j: j
    return pl.pallas_call(
        functools.partial(_lru_kernel, n_ctx_blocks=n_ctx_blocks, n_blocks=n_blocks),
        grid=(n_cc, n_blocks),
        in_specs=[
            cur(ident), prev(ident), nxt(ident), cur(bwd_block), prev(bwd_block), nxt(bwd_block),
            pl.BlockSpec((CONV_W, LRU_CW), lambda c, j: (0, c)),
            pl.BlockSpec((1, LRU_CW), lambda c, j: (0, c)),
            pl.BlockSpec((2, None, LRU_CW, 2 * LRU_CW), lambda c, j: (0, c, 0, 0)),
            pl.BlockSpec((2, None, 1, 2 * LRU_CW), lambda c, j: (0, c, 0, 0)),
            pl.BlockSpec((2, None, 1, LRU_CW), lambda c, j: (0, c, 0, 0)),
        ],
        out_specs=[
            pl.BlockSpec((rows, LRU_CW), lambda c, j: (j, c)),
            pl.BlockSpec((rows, LRU_CW), lambda c, j: (bwd_block(j), c)),
        ],
        out_shape=[jax.ShapeDtypeStruct((n_rows, D_MODEL), F32)] * 2,
        scratch_shapes=[
            pltpu.VMEM((2, rows, LRU_CW), F32),
            pltpu.VMEM((2, rows, LRU_CW), F32),
            pltpu.VMEM((2, BATCH, LRU_CW), F32),
        ],
        compiler_params=_cparams("parallel", "arbitrary"),
        name="rg_lru",
    )(z, z, z, z, z, z, conv_w, conv_b, gate_w, gate_b, lam)


def _gelu_tanh(x):
    return 0.5 * x * (1.0 + jnp.tanh(0.7978845608028654 * (x + 0.044715 * (x * x * x))))


def _merge_kernel(x_ref, mod_ref, og_ref, hf_ref, hb_ref, gl0_ref, gl1_ref, ga0_ref, ga1_ref,
                  gb0_ref, gb1_ref, wr_ref, wl_ref, wo_ref, o_ref):
    rows = x_ref.shape[0]
    y_a = jnp.dot(og_ref[...], wr_ref[...], preferred_element_type=F32)
    h = hf_ref[...] + hb_ref[...]
    gl = jnp.concatenate([gl0_ref[...], gl1_ref[...]], axis=1).astype(F32)
    y_b = jnp.dot((h * _gelu_tanh(gl)).astype(BF16), wl_ref[...], preferred_element_type=F32)
    ga = jnp.concatenate([ga0_ref[...], ga1_ref[...]], axis=1).astype(F32)
    gb = jnp.concatenate([gb0_ref[...], gb1_ref[...]], axis=1).astype(F32)
    m = (_sigmoid(ga) * y_a + _sigmoid(gb) * y_b).astype(BF16)
    y = jnp.dot(m, wo_ref[...], preferred_element_type=F32)
    y3 = y.reshape(rows // BATCH, BATCH, D_MODEL)
    x3 = x_ref[...].reshape(rows // BATCH, BATCH, D_MODEL)
    o_ref[...] = (x3 + mod_ref[2][None] * y3).reshape(rows, D_MODEL)


def _merge(xs, tab, og, hf, hb, z, w_ret_o, w_lru_o, w_out, ctx_rows, row_start):
    n = xs.shape[0]
    tm = ROW_TILE
    t0 = row_start // tm

    def zc(c):
        return pl.BlockSpec((None, tm, PROJ_CHUNK), lambda i: (c, i + t0, 0))

    row = lambda w: pl.BlockSpec((tm, w), lambda i: (i + t0, 0))
    return pl.pallas_call(
        _merge_kernel,
        grid=(n // tm - t0,),
        in_specs=[
            row(D_MODEL),
            pl.BlockSpec((None, 3, BATCH, D_MODEL),
                         lambda i: (((i + t0) * tm >= ctx_rows).astype(jnp.int32), 1, 0, 0)),
            row(D_MODEL), row(D_MODEL), row(D_MODEL),
            zc(6), zc(7), zc(8), zc(9), zc(10), zc(11),
            _resident((D_MODEL, D_MODEL)), _resident((D_MODEL, D_MODEL)), _resident((D_MODEL, D_MODEL)),
        ],
        out_specs=row(D_MODEL),
        out_shape=jax.ShapeDtypeStruct((n, D_MODEL), F32),
        compiler_params=_cparams("parallel"),
        name="merge_out_proj",
    )(xs, tab, og, hf, hb, z, z, z, z, z, z, w_ret_o, w_lru_o, w_out)


def _final_kernel(x_ref, g_ref, o_ref):
    x = x_ref[...]
    ms = jnp.mean(x * x, axis=-1, keepdims=True)
    o_ref[...] = (x * lax.rsqrt(ms + EPS)) * g_ref[...]


def _final_norm(xs, g, t_all, t_ctx):
    tt = LRU_TB
    t_lat = t_all - t_ctx
    x2 = xs.reshape(t_all, BATCH * D_MODEL)
    return pl.pallas_call(
        _final_kernel,
        grid=(BATCH, t_lat // tt),
        in_specs=[
            pl.BlockSpec((tt, D_MODEL), lambda b, t: (t + t_ctx // tt, b)),
            pl.BlockSpec((1, D_MODEL), lambda b, t: (0, 0)),
        ],
        out_specs=pl.BlockSpec((None, tt, D_MODEL), lambda b, t: (b, t, 0)),
        out_shape=jax.ShapeDtypeStruct((BATCH, t_lat, D_MODEL), F32),
        compiler_params=_cparams("parallel", "parallel"),
        name="final_norm",
    )(x2, g)


def _rotary_tables(t_ctx, t_lat):
    pos = jnp.arange(t_lat, dtype=jnp.int32)
    row = (pos // GRID_W).astype(F32)
    col = (pos % GRID_W).astype(F32)
    n_f = RET_DK // 4
    inv = ROPE_BASE ** (-jnp.arange(n_f, dtype=F32) / n_f)
    ang = jnp.concatenate([row[:, None] * inv, col[:, None] * inv], axis=-1)
    cos = jnp.concatenate([jnp.ones((t_ctx, RET_DK // 2), F32), jnp.cos(ang)], axis=0)
    sin = jnp.concatenate([jnp.zeros((t_ctx, RET_DK // 2), F32), jnp.sin(ang)], axis=0)
    cos128 = jnp.tile(cos, (1, 4))
    sin128 = jnp.tile(jnp.concatenate([-sin, sin], axis=1), (1, 2))
    return jnp.repeat(cos128, BATCH, axis=0), jnp.repeat(sin128, BATCH, axis=0)


def _ffn_weights(w_gu, w_down):
    nf = FFN_HIDDEN // FFN_TF
    wg = w_gu[:, :FFN_HIDDEN].astype(BF16).reshape(D_MODEL, nf, FFN_TF).transpose(1, 0, 2)
    wu = w_gu[:, FFN_HIDDEN:].astype(BF16).reshape(D_MODEL, nf, FFN_TF).transpose(1, 0, 2)
    wd = w_down.astype(BF16).reshape(nf, FFN_TF, D_MODEL)
    return wg, wu, wd


def _lru_gate_weights(gate_w, gate_b):
    n_cc = D_MODEL // LRU_CW
    per = LRU_CW // LRU_BW
    w = gate_w.reshape(2, 2, n_cc, per, LRU_BW, LRU_BW)
    eye = jnp.eye(per, dtype=gate_w.dtype)
    bd = jnp.einsum('dgcpij,pq->dgcpiqj', w, eye).reshape(2, 2, n_cc, LRU_CW, LRU_CW)
    wcat = jnp.concatenate([bd[:, 0], bd[:, 1]], axis=-1).astype(BF16)
    b = gate_b.reshape(2, 2, n_cc, 1, LRU_CW)
    bcat = jnp.concatenate([b[:, 0], b[:, 1]], axis=-1)
    return wcat, bcat


def kernel(x, c, ctx, c_ctx, w_mod, b_mod, norm_g, ffn1_w_gu, ffn1_w_down, ffn2_w_gu, ffn2_w_down,
           w_in, ret_decay_logit, w_ret_o, lru_conv_w, lru_conv_b, lru_gate_w, lru_gate_b,
           lru_lambda, w_lru_o, w_out, final_g):
    depth = w_mod.shape[0]
    t_lat, t_ctx = x.shape[1], ctx.shape[1]
    t_all = t_lat + t_ctx
    n_rows = t_all * BATCH
    ctx_rows = t_ctx * BATCH
    assert x.shape[0] == BATCH and x.shape[2] == D_MODEL
    assert t_ctx % LRU_TB == 0 and t_lat % LRU_TB == 0 and ctx_rows % ROW_TILE == 0

    xs = jnp.concatenate([ctx, x], axis=1).transpose(1, 0, 2).reshape(n_rows, D_MODEL)
    tabs = _mod_tables(c, c_ctx, w_mod, b_mod)
    cos_t, sin_t = _rotary_tables(t_ctx, t_lat)
    n_cc = D_MODEL // LRU_CW

    for l in range(depth):
        last = l == depth - 1
        tab = tabs[l]
        g = norm_g[l].reshape(3, 1, D_MODEL)
        xs = _ffn_sublayer(xs, tab, 0, g[0], *_ffn_weights(ffn1_w_gu[l], ffn1_w_down[l]), ctx_rows, 0)

        w_in_c = w_in[l].astype(BF16).reshape(D_MODEL, N_PROJ_CHUNKS, PROJ_CHUNK).transpose(1, 0, 2)
        q, k, z = _in_projection(xs, tab, g[1], cos_t, sin_t, w_in_c, ctx_rows)
        logit_rows = jnp.broadcast_to(ret_decay_logit[l].astype(F32)[:, :, None, None],
                                      (2, RET_HEADS, 1, 128))
        og = _retention(q, k, z, logit_rows, t_all, t_ctx // RET_CHUNK)
        gw, gb = _lru_gate_weights(lru_gate_w[l], lru_gate_b[l])
        hf, hb = _rg_lru(z, lru_conv_w[l], lru_conv_b[l].reshape(1, D_MODEL), gw, gb,
                         lru_lambda[l].reshape(2, n_cc, 1, LRU_CW), n_rows, t_ctx // LRU_TB)
        row_start = ctx_rows if last else 0
        xs = _merge(xs, tab, og, hf, hb, z, w_ret_o[l].astype(BF16), w_lru_o[l].astype(BF16),
                    w_out[l].astype(BF16), ctx_rows, row_start)
        xs = _ffn_sublayer(xs, tab, 6, g[2], *_ffn_weights(ffn2_w_gu[l], ffn2_w_down[l]),
                           ctx_rows, row_start)
    return _final_norm(xs, final_g.reshape(1, D_MODEL), t_all, t_ctx)
```

```python
import functools

import jax
import jax.numpy as jnp
from jax import lax
from jax.experimental import pallas as pl
from jax.experimental.pallas import tpu as pltpu

F32 = jnp.float32
BF16 = jnp.bfloat16

D_MODEL = 1024
BATCH = 8
LANES = 128
RET_HEADS = 8
RET_DK = 64
RET_DV = 128
RET_CHUNK = 128
HEAD_GROUP = 4
GRID_W = 64
ROPE_BASE = 10000.0
LRU_BLOCKS = 16
LRU_BW = D_MODEL // LRU_BLOCKS
LRU_C = 8.0
LRU_CW = 256
LRU_TB = 128
CONV_W = 4
FFN_HIDDEN = 2816
FFN_TF = 256
FFN_RES = 0.5
N_MOD = 9
EPS = 1e-6
PROJ_CHUNK = 512
N_PROJ_CHUNKS = 14
ROW_TILE = 512
VMEM_LIMIT = 52 * 1024 * 1024


def _cparams(*sem):
    return pltpu.CompilerParams(dimension_semantics=sem, vmem_limit_bytes=VMEM_LIMIT)


def _resident(shape):
    return pl.BlockSpec(shape, lambda *_: (0,) * len(shape), pipeline_mode=pl.Buffered(1))


def _sigmoid(x):
    return 1.0 / (1.0 + jnp.exp(-x))


def _ada_norm(x, g, shift, scale):
    rows = x.shape[0]
    ms = jnp.mean(x * x, axis=-1, keepdims=True)
    y = (x * lax.rsqrt(ms + EPS)) * g
    y3 = y.reshape(rows // BATCH, BATCH, D_MODEL)
    h = y3 * (1.0 + scale)[None] + shift[None]
    return h.reshape(rows, D_MODEL)


def _mod_kernel(c_ref, w_ref, b_ref, o_ref):
    c = c_ref[...]
    s = (c * _sigmoid(c)).astype(BF16)
    o_ref[...] = jnp.dot(s, w_ref[...].astype(BF16), preferred_element_type=F32) + b_ref[...]


def _mod_tables(c, c_ctx, w_mod, b_mod):
    depth = w_mod.shape[0]
    cc = jnp.zeros((2 * BATCH, D_MODEL), F32).at[:BATCH].set(c).at[BATCH].set(c_ctx)
    out = pl.pallas_call(
        _mod_kernel,
        grid=(depth, N_MOD),
        in_specs=[
            pl.BlockSpec((2 * BATCH, D_MODEL), lambda l, j: (0, 0)),
            pl.BlockSpec((None, D_MODEL, D_MODEL), lambda l, j: (l, 0, j)),
            pl.BlockSpec((None, 1, D_MODEL), lambda l, j: (l, 0, j)),
        ],
        out_specs=pl.BlockSpec((None, 2 * BATCH, D_MODEL), lambda l, j: (l, 0, j)),
        out_shape=jax.ShapeDtypeStruct((depth, 2 * BATCH, N_MOD * D_MODEL), F32),
        compiler_params=_cparams("parallel", "parallel"),
        name="adaln_mod",
    )(cc, w_mod, b_mod.reshape(depth, 1, N_MOD * D_MODEL))
    out = out.reshape(depth, 2 * BATCH, N_MOD, D_MODEL)
    lat = out[:, :BATCH].transpose(0, 2, 1, 3)
    ctx = jnp.broadcast_to(out[:, BATCH][:, :, None, :], lat.shape)
    return jnp.stack([ctx, lat], axis=1)


def _ffn_kernel(x_ref, mod_ref, g_ref, wg_ref, wu_ref, wd_ref, o_ref, h_scr, acc_scr):
    rows = x_ref.shape[0]
    x = x_ref[...]
    h_scr[...] = _ada_norm(x, g_ref[...], mod_ref[0], mod_ref[1]).astype(BF16)
    acc_scr[...] = jnp.zeros_like(acc_scr)

    def body(k, carry):
        hb = h_scr[...]
        u = jnp.dot(hb, wg_ref[k], preferred_element_type=F32)
        v = jnp.dot(hb, wu_ref[k], preferred_element_type=F32)
        a = ((u * _sigmoid(u)) * v).astype(BF16)
        acc_scr[...] += jnp.dot(a, wd_ref[k], preferred_element_type=F32)
        return carry

    lax.fori_loop(0, wg_ref.shape[0], body, 0)
    y3 = acc_scr[...].reshape(rows // BATCH, BATCH, D_MODEL)
    x3 = x.reshape(rows // BATCH, BATCH, D_MODEL)
    o_ref[...] = (x3 + (FFN_RES * mod_ref[2])[None] * y3).reshape(rows, D_MODEL)


def _ffn_sublayer(xs, tab, sub, g, wg, wu, wd, ctx_rows, row_start):
    n = xs.shape[0]
    tm = ROW_TILE
    t0 = row_start // tm
    nf = wg.shape[0]
    return pl.pallas_call(
        _ffn_kernel,
        grid=(n // tm - t0,),
        in_specs=[
            pl.BlockSpec((tm, D_MODEL), lambda i: (i + t0, 0)),
            pl.BlockSpec((None, 3, BATCH, D_MODEL),
                         lambda i: (((i + t0) * tm >= ctx_rows).astype(jnp.int32), sub // 3, 0, 0)),
            _resident((1, D_MODEL)),
            _resident((nf, D_MODEL, FFN_TF)),
            _resident((nf, D_MODEL, FFN_TF)),
            _resident((nf, FFN_TF, D_MODEL)),
        ],
        out_specs=pl.BlockSpec((tm, D_MODEL), lambda i: (i + t0, 0)),
        out_shape=jax.ShapeDtypeStruct((n, D_MODEL), F32),
        scratch_shapes=[pltpu.VMEM((tm, D_MODEL), BF16), pltpu.VMEM((tm, D_MODEL), F32)],
        compiler_params=_cparams("parallel"),
        name="ffn_sublayer",
    )(xs, tab, g, wg, wu, wd)


def _swap_halves(a):
    w = a.shape[1]
    lane = lax.broadcasted_iota(jnp.int32, a.shape, 1)
    first_half = (lane % RET_DK) < (RET_DK // 2)
    return jnp.where(first_half, pltpu.roll(a, w - RET_DK // 2, 1), pltpu.roll(a, RET_DK // 2, 1))


def _rows_to_sequences(a, slab_scr, out_ref):
    rows, width = a.shape
    steps = rows // BATCH
    n_slabs = width // LANES
    for s in range(n_slabs):
        slab_scr[s] = a[:, s * LANES:(s + 1) * LANES]
    for b in range(BATCH):
        for s in range(n_slabs):
            col = b * width + s * LANES
            out_ref[:, col:col + LANES] = slab_scr[s, pl.ds(b, steps, stride=BATCH), :].astype(out_ref.dtype)


def _inproj_kernel(x_ref, mod_ref, g_ref, cos_ref, sin_ref, w_ref, q_ref, k_ref, v_ref, z_ref,
                   h_scr, slab_scr):
    h_scr[...] = _ada_norm(x_ref[...], g_ref[...], mod_ref[0], mod_ref[1]).astype(BF16)
    reps = PROJ_CHUNK // cos_ref.shape[1]
    cos = jnp.tile(cos_ref[...], (1, reps))
    sin = jnp.tile(sin_ref[...], (1, reps))

    def rotate(a):
        return a * cos + _swap_halves(a) * sin

    q = jnp.dot(h_scr[...], w_ref[0], preferred_element_type=F32)
    _rows_to_sequences(rotate(q), slab_scr, q_ref)
    k = jnp.dot(h_scr[...], w_ref[1], preferred_element_type=F32)
    _rows_to_sequences(rotate(k) * (RET_DK ** -0.5), slab_scr, k_ref)
    for c in range(2):
        v = jnp.dot(h_scr[...], w_ref[2 + c], preferred_element_type=F32)
        _rows_to_sequences(v, slab_scr, v_ref.at[c])

    def body(c, carry):
        z_ref[c] = jnp.dot(h_scr[...], w_ref[c + 4], preferred_element_type=F32).astype(BF16)
        return carry

    lax.fori_loop(0, N_PROJ_CHUNKS - 4, body, 0)


def _in_projection(xs, tab, g, cos_t, sin_t, w_in, ctx_rows):
    n = xs.shape[0]
    tm = ROW_TILE
    nz = N_PROJ_CHUNKS - 4
    steps = tm // BATCH
    t_all = n // BATCH
    seq_w = BATCH * PROJ_CHUNK
    return pl.pallas_call(
        _inproj_kernel,
        grid=(n // tm,),
        in_specs=[
            pl.BlockSpec((tm, D_MODEL), lambda i: (i, 0)),
            pl.BlockSpec((None, 3, BATCH, D_MODEL),
                         lambda i: ((i * tm >= ctx_rows).astype(jnp.int32), 1, 0, 0)),
            _resident((1, D_MODEL)),
            pl.BlockSpec((tm, LANES), lambda i: (i, 0)),
            pl.BlockSpec((tm, LANES), lambda i: (i, 0)),
            _resident((N_PROJ_CHUNKS, D_MODEL, PROJ_CHUNK)),
        ],
        out_specs=[
            pl.BlockSpec((steps, seq_w), lambda i: (i, 0)),
            pl.BlockSpec((steps, seq_w), lambda i: (i, 0)),
            pl.BlockSpec((2, steps, seq_w), lambda i: (0, i, 0)),
            pl.BlockSpec((nz, tm, PROJ_CHUNK), lambda i: (0, i, 0)),
        ],
        out_shape=[
            jax.ShapeDtypeStruct((t_all, seq_w), BF16),
            jax.ShapeDtypeStruct((t_all, seq_w), BF16),
            jax.ShapeDtypeStruct((2, t_all, seq_w), BF16),
            jax.ShapeDtypeStruct((nz, n, PROJ_CHUNK), BF16),
        ],
        scratch_shapes=[pltpu.VMEM((tm, D_MODEL), BF16),
                        pltpu.VMEM((PROJ_CHUNK // LANES, tm, LANES), F32)],
        compiler_params=_cparams("parallel"),
        name="in_projection",
    )(xs, tab, g, cos_t, sin_t, w_in)


def _log_sigmoid(x):
    return jnp.minimum(x, 0.0) - jnp.log1p(jnp.exp(-jnp.abs(x)))


def _dot_t1(a, b):
    return lax.dot_general(a, b, (((1,), (1,)), ((), ())), preferred_element_type=F32)


def _retention_kernel(lg_ref, q_ref, k_ref, v_ref, o_ref,
                      dmat, qdec, kdec, cdec, sb_scr, sf_scr, *, n_ctx_chunks):
    c_len = RET_CHUNK
    pair_k = 2 * RET_DK
    pair_v = 2 * RET_DV
    n_pairs = HEAD_GROUP // 2
    n_chunks = q_ref.shape[0] // c_len
    hg = pl.program_id(1)
    row_i = lax.broadcasted_iota(jnp.int32, (c_len, c_len), 0)
    col_i = lax.broadcasted_iota(jnp.int32, (c_len, c_len), 1)
    row = row_i.astype(F32)
    col = col_i.astype(F32)
    rel = row - col
    first_lanes = col_i < RET_DK
    first_rows = row_i < RET_DK
    own_block = (lax.broadcasted_iota(jnp.int32, (pair_k, pair_v), 0) < RET_DK) == (
        lax.broadcasted_iota(jnp.int32, (pair_k, pair_v), 1) < RET_DV)

    for p in range(n_pairs):
        h0 = hg * HEAD_GROUP + 2 * p
        lg = [[_log_sigmoid(lg_ref[d, h0 + e]) for e in range(2)] for d in range(2)]
        for e in range(2):
            dmat[2 * p + e] = jnp.where(rel >= 0.0, jnp.exp(lg[0][e] * jnp.maximum(rel, 0.0)),
                                        jnp.exp(lg[1][e] * jnp.maximum(-rel, 0.0)))
        lane_f = jnp.where(first_lanes, lg[0][0], lg[0][1])
        lane_b = jnp.where(first_lanes, lg[1][0], lg[1][1])
        row_f = jnp.where(first_rows, lg[0][0], lg[0][1])
        row_b = jnp.where(first_rows, lg[1][0], lg[1][1])
        qdec[0, p] = jnp.exp(lane_f * (row + 1.0))
        qdec[1, p] = jnp.exp(lane_b * (c_len - row))
        kdec[0, p] = jnp.exp(row_f * (c_len - 1.0 - col))
        kdec[1, p] = jnp.exp(row_b * col)
        cdec[0, p] = jnp.exp(row_f * c_len)
        cdec[1, p] = jnp.exp(row_b * c_len)

    def advance(d, p, s, kp, vp):
        kd = (kp.astype(F32).T * kdec[d, p]).astype(BF16)
        upd = jnp.dot(kd, vp, preferred_element_type=F32)
        cd = cdec[d, p]
        return s * jnp.concatenate([cd, cd], axis=1) + jnp.where(own_block, upd, 0.0)

    sf_scr[...] = jnp.zeros_like(sf_scr)

    def bwd_body(i, carry):
        c = jnp.where(i < n_ctx_chunks, n_ctx_chunks - 1 - i, n_chunks - 1 + n_ctx_chunks - i)
        rows = pl.ds(pl.multiple_of(c * c_len, c_len), c_len)
        for p in range(n_pairs):
            kp = k_ref[rows, p * pair_k:(p + 1) * pair_k]
            vp = v_ref[rows, p * pair_v:(p + 1) * pair_v]
            s = sf_scr[p]
            sb_scr[c, p] = s.astype(BF16)
            sf_scr[p] = advance(1, p, s, kp, vp)
        return carry

    lax.fori_loop(0, n_chunks, bwd_body, 0)

    sf_scr[...] = jnp.zeros_like(sf_scr)
    zero_v = jnp.zeros((c_len, RET_DV), BF16)
    zero_q = jnp.zeros((c_len, pair_k), BF16)

    def fwd_body(c, carry):
        rows = pl.ds(pl.multiple_of(c * c_len, c_len), c_len)
        for p in range(n_pairs):
            qp = q_ref[rows, p * pair_k:(p + 1) * pair_k]
            kp = k_ref[rows, p * pair_k:(p + 1) * pair_k]
            vp = v_ref[rows, p * pair_v:(p + 1) * pair_v]
            q_split = jnp.concatenate([jnp.where(first_lanes, qp, zero_q),
                                       jnp.where(first_lanes, zero_q, qp)], axis=0)
            scores = _dot_t1(q_split, kp)
            pa = (scores[:c_len] * dmat[2 * p]).astype(BF16)
            pb = (scores[c_len:] * dmat[2 * p + 1]).astype(BF16)
            qf = qp.astype(F32)
            s = sf_scr[p]
            lhs = jnp.concatenate([pa, pb, (qf * qdec[0, p]).astype(BF16), (qf * qdec[1, p]).astype(BF16)],
                                  axis=1)
            rhs = jnp.concatenate([
                jnp.concatenate([vp[:, :RET_DV], zero_v], axis=1),
                jnp.concatenate([zero_v, vp[:, RET_DV:]], axis=1),
                s.astype(BF16), sb_scr[c, p]], axis=0)
            o = jnp.dot(lhs, rhs, preferred_element_type=F32)
            sf_scr[p] = advance(0, p, s, kp, vp)
            for e in range(2):
                oe = o[:, e * RET_DV:(e + 1) * RET_DV]
                mu = jnp.mean(oe, axis=-1, keepdims=True)
                d = oe - mu
                var = jnp.mean(d * d, axis=-1, keepdims=True)
                col0 = p * pair_v + e * RET_DV
                o_ref[rows, col0:col0 + RET_DV] = (d * lax.rsqrt(var + EPS)).astype(BF16)
        return carry

    lax.fori_loop(0, n_chunks, fwd_body, 0)


def _retention(q, k, v, logit_rows, n_ctx_chunks):
    t_all = q.shape[0]
    n_groups = RET_HEADS // HEAD_GROUP
    n_pairs = HEAD_GROUP // 2
    qw = HEAD_GROUP * RET_DK
    vw = HEAD_GROUP * RET_DV
    assert vw == PROJ_CHUNK and n_groups == v.shape[0]
    n_chunks = t_all // RET_CHUNK
    return pl.pallas_call(
        functools.partial(_retention_kernel, n_ctx_chunks=n_ctx_chunks),
        grid=(BATCH, n_groups),
        in_specs=[
            _resident((2, RET_HEADS, 1, LANES)),
            pl.BlockSpec((t_all, qw), lambda b, h: (0, b * n_groups + h)),
            pl.BlockSpec((t_all, qw), lambda b, h: (0, b * n_groups + h)),
            pl.BlockSpec((None, t_all, vw), lambda b, h: (h, 0, b)),
        ],
        out_specs=pl.BlockSpec((t_all, vw), lambda b, h: (0, b * n_groups + h)),
        out_shape=jax.ShapeDtypeStruct((t_all, BATCH * RET_HEADS * RET_DV), BF16),
        scratch_shapes=[
            pltpu.VMEM((HEAD_GROUP, RET_CHUNK, RET_CHUNK), F32),
            pltpu.VMEM((2, n_pairs, RET_CHUNK, 2 * RET_DK), F32),
            pltpu.VMEM((2, n_pairs, 2 * RET_DK, RET_CHUNK), F32),
            pltpu.VMEM((2, n_pairs, 2 * RET_DK, RET_CHUNK), F32),
            pltpu.VMEM((n_chunks, n_pairs, 2 * RET_DK, 2 * RET_DV), BF16),
            pltpu.VMEM((n_pairs, 2 * RET_DK, 2 * RET_DV), F32),
        ],
        compiler_params=_cparams("parallel", "parallel"),
        name="retention",
    )(logit_rows, q, k, v)


def _lru_kernel(xf_ref, xfp_ref, xfn_ref, xb_ref, xbp_ref, xbn_ref,
                cw_ref, cb_ref, gw_ref, gb_ref, lam_ref, hf_ref, hb_ref,
                a_scr, b_scr, h_scr, *, n_ctx_blocks, n_blocks):
    j = pl.program_id(1)
    jb = jnp.where(j < n_ctx_blocks, n_ctx_blocks - 1 - j, n_blocks - 1 + n_ctx_blocks - j)
    rows = xf_ref.shape[0]
    halo = xfp_ref.shape[0]
    cw = cw_ref[...]
    cb = cb_ref[...]

    def prepare(d, blk, x_ref, xp_ref, xn_ref):
        is_first = jnp.logical_or(blk == 0, blk == n_ctx_blocks)
        is_last = jnp.logical_or(blk == n_ctx_blocks - 1, blk == n_blocks - 1)
        prev = jnp.where(is_first, 0.0, xp_ref[...].astype(F32))
        nxt = jnp.where(is_last, 0.0, xn_ref[...].astype(F32))
        xe = jnp.concatenate([prev, x_ref[...].astype(F32), nxt], axis=0)
        u = cb
        for tap in range(CONV_W):
            off = halo - (2 - tap) * BATCH
            u = u + xe[off:off + rows] * cw[tap:tap + 1]
        g = jnp.dot(u.astype(BF16), gw_ref[d], preferred_element_type=F32) + gb_ref[d]
        r = _sigmoid(g[:, :LRU_CW])
        i = _sigmoid(g[:, LRU_CW:])
        lam = lam_ref[d]
        softplus = jnp.maximum(-lam, 0.0) + jnp.log1p(jnp.exp(-jnp.abs(lam)))
        log_a = (-LRU_C * r) * softplus
        th = jnp.tanh(log_a)
        one_minus_a2 = (-2.0 * th) / (1.0 - th)
        a_scr[d] = jnp.exp(log_a)
        b_scr[d] = jnp.sqrt(one_minus_a2) * (i * u)

    prepare(0, j, xf_ref, xfp_ref, xfn_ref)
    prepare(1, jb, xb_ref, xbp_ref, xbn_ref)

    @pl.when(j == 0)
    def _():
        h_scr[...] = jnp.zeros_like(h_scr)

    n_steps = rows // BATCH

    def step(s, carry):
        hf, hb = carry
        rf = pl.ds(pl.multiple_of(s * BATCH, BATCH), BATCH)
        hf = a_scr[0, rf, :] * hf + b_scr[0, rf, :]
        hf_ref[rf, :] = hf
        rb = pl.ds(pl.multiple_of((n_steps - 1 - s) * BATCH, BATCH), BATCH)
        hb = a_scr[1, rb, :] * hb + b_scr[1, rb, :]
        hb_ref[rb, :] = hb
        return hf, hb

    hf, hb = lax.fori_loop(0, n_steps, step, (h_scr[0], h_scr[1]), unroll=8)
    h_scr[0] = hf
    h_scr[1] = hb


def _rg_lru(z, conv_w, conv_b, gate_w, gate_b, lam, n_rows, n_ctx_blocks):
    rows = LRU_TB * BATCH
    n_blocks = n_rows // rows
    n_cc = D_MODEL // LRU_CW
    per_chunk = PROJ_CHUNK // LRU_CW
    halo = 16
    hb_per_block = rows // halo
    n_halo = n_rows // halo

    def bwd_block(j):
        return jnp.where(j < n_ctx_blocks, n_ctx_blocks - 1 - j, n_blocks - 1 + n_ctx_blocks - j)

    def cur(f):
        return pl.BlockSpec((None, rows, LRU_CW), lambda c, j: (2 + c // per_chunk, f(j), c % per_chunk))

    def prev(f):
        return pl.BlockSpec((None, halo, LRU_CW),
                            lambda c, j: (2 + c // per_chunk, jnp.maximum(f(j) * hb_per_block - 1, 0),
                                          c % per_chunk))

    def nxt(f):
        return pl.BlockSpec((None, halo, LRU_CW),
                            lambda c, j: (2 + c // per_chunk,
                                          jnp.minimum((f(j) + 1) * hb_per_block, n_halo - 1),
                                          c % per_chunk))

    ident = lambda j: j
    return pl.pallas_call(
        functools.partial(_lru_kernel, n_ctx_blocks=n_ctx_blocks, n_blocks=n_blocks),
        grid=(n_cc, n_blocks),
        in_specs=[
            cur(ident), prev(ident), nxt(ident), cur(bwd_block), prev(bwd_block), nxt(bwd_block),
            pl.BlockSpec((CONV_W, LRU_CW), lambda c, j: (0, c)),
            pl.BlockSpec((1, LRU_CW), lambda c, j: (0, c)),
            pl.BlockSpec((2, None, LRU_CW, 2 * LRU_CW), lambda c, j: (0, c, 0, 0)),
            pl.BlockSpec((2, None, 1, 2 * LRU_CW), lambda c, j: (0, c, 0, 0)),
            pl.BlockSpec((2, None, 1, LRU_CW), lambda c, j: (0, c, 0, 0)),
        ],
        out_specs=[
            pl.BlockSpec((rows, LRU_CW), lambda c, j: (j, c)),
            pl.BlockSpec((rows, LRU_CW), lambda c, j: (bwd_block(j), c)),
        ],
        out_shape=[jax.ShapeDtypeStruct((n_rows, D_MODEL), F32)] * 2,
        scratch_shapes=[
            pltpu.VMEM((2, rows, LRU_CW), F32),
            pltpu.VMEM((2, rows, LRU_CW), F32),
            pltpu.VMEM((2, BATCH, LRU_CW), F32),
        ],
        compiler_params=_cparams("parallel", "arbitrary"),
        name="rg_lru",
    )(z, z, z, z, z, z, conv_w, conv_b, gate_w, gate_b, lam)


def _gelu_tanh(x):
    return 0.5 * x * (1.0 + jnp.tanh(0.7978845608028654 * (x + 0.044715 * (x * x * x))))


def _sequences_to_rows(o_ref, slab_scr):
    steps = o_ref.shape[0]
    n_slabs = slab_scr.shape[0]
    width = n_slabs * LANES
    for b in range(BATCH):
        for s in range(n_slabs):
            col = b * width + s * LANES
            slab_scr[s, pl.ds(b, steps, stride=BATCH), :] = o_ref[:, col:col + LANES].astype(F32)
    return jnp.concatenate([slab_scr[s] for s in range(n_slabs)], axis=1)


def _merge_kernel(x_ref, mod_ref, o_ref_in, hf_ref, hb_ref, gr0_ref, gr1_ref, gl0_ref, gl1_ref,
                  ga0_ref, ga1_ref, gb0_ref, gb1_ref, wr_ref, wl_ref, wo_ref, o_ref, slab_scr):
    rows = x_ref.shape[0]

    def both(r0, r1):
        return jnp.concatenate([r0[...], r1[...]], axis=1).astype(F32)

    gr = both(gr0_ref, gr1_ref)
    o_ret = _sequences_to_rows(o_ref_in, slab_scr) * (gr * _sigmoid(gr))
    y_a = jnp.dot(o_ret.astype(BF16), wr_ref[...], preferred_element_type=F32)
    h = hf_ref[...] + hb_ref[...]
    y_b = jnp.dot((h * _gelu_tanh(both(gl0_ref, gl1_ref))).astype(BF16), wl_ref[...],
                  preferred_element_type=F32)
    m = (_sigmoid(both(ga0_ref, ga1_ref)) * y_a + _sigmoid(both(gb0_ref, gb1_ref)) * y_b).astype(BF16)
    y = jnp.dot(m, wo_ref[...], preferred_element_type=F32)
    y3 = y.reshape(rows // BATCH, BATCH, D_MODEL)
    x3 = x_ref[...].reshape(rows // BATCH, BATCH, D_MODEL)
    o_ref[...] = (x3 + mod_ref[2][None] * y3).reshape(rows, D_MODEL)


def _merge(xs, tab, o_seq, hf, hb, z, w_ret_o, w_lru_o, w_out, ctx_rows, row_start):
    n = xs.shape[0]
    tm = ROW_TILE
    t0 = row_start // tm

    def zc(c):
        return pl.BlockSpec((None, tm, PROJ_CHUNK), lambda i: (c, i + t0, 0))

    row = lambda w: pl.BlockSpec((tm, w), lambda i: (i + t0, 0))
    return pl.pallas_call(
        _merge_kernel,
        grid=(n // tm - t0,),
        in_specs=[
            row(D_MODEL),
            pl.BlockSpec((None, 3, BATCH, D_MODEL),
                         lambda i: (((i + t0) * tm >= ctx_rows).astype(jnp.int32), 1, 0, 0)),
            pl.BlockSpec((tm // BATCH, BATCH * D_MODEL), lambda i: (i + t0, 0)),
            row(D_MODEL), row(D_MODEL),
            zc(0), zc(1), zc(4), zc(5), zc(6), zc(7), zc(8), zc(9),
            _resident((D_MODEL, D_MODEL)), _resident((D_MODEL, D_MODEL)), _resident((D_MODEL, D_MODEL)),
        ],
        out_specs=row(D_MODEL),
        out_shape=jax.ShapeDtypeStruct((n, D_MODEL), F32),
        scratch_shapes=[pltpu.VMEM((D_MODEL // LANES, tm, LANES), F32)],
        compiler_params=_cparams("parallel"),
        name="merge_out_proj",
    )(xs, tab, o_seq, hf, hb, z, z, z, z, z, z, z, z, w_ret_o, w_lru_o, w_out)


def _final_kernel(x_ref, g_ref, o_ref):
    x = x_ref[...]
    ms = jnp.mean(x * x, axis=-1, keepdims=True)
    o_ref[...] = (x * lax.rsqrt(ms + EPS)) * g_ref[...]


def _final_norm(xs, g, t_all, t_ctx):
    t_lat = t_all - t_ctx
    tt = max(t for t in (LRU_TB, 2 * LRU_TB, 4 * LRU_TB) if t_ctx % t == 0 and t_lat % t == 0)
    x2 = xs.reshape(t_all, BATCH * D_MODEL)
    return pl.pallas_call(
        _final_kernel,
        grid=(BATCH, t_lat // tt),
        in_specs=[
            pl.BlockSpec((tt, D_MODEL), lambda b, t: (t + t_ctx // tt, b)),
            pl.BlockSpec((1, D_MODEL), lambda b, t: (0, 0)),
        ],
        out_specs=pl.BlockSpec((None, tt, D_MODEL), lambda b, t: (b, t, 0)),
        out_shape=jax.ShapeDtypeStruct((BATCH, t_lat, D_MODEL), F32),
        compiler_params=_cparams("parallel", "parallel"),
        name="final_norm",
    )(x2, g)


def _rotary_tables(t_ctx, t_lat):
    pos = jnp.arange(t_lat, dtype=jnp.int32)
    row = (pos // GRID_W).astype(F32)
    col = (pos % GRID_W).astype(F32)
    n_f = RET_DK // 4
    inv = ROPE_BASE ** (-jnp.arange(n_f, dtype=F32) / n_f)
    ang = jnp.concatenate([row[:, None] * inv, col[:, None] * inv], axis=-1)
    cos = jnp.concatenate([jnp.ones((t_ctx, RET_DK // 2), F32), jnp.cos(ang)], axis=0)
    sin = jnp.concatenate([jnp.zeros((t_ctx, RET_DK // 2), F32), jnp.sin(ang)], axis=0)
    cos128 = jnp.tile(cos, (1, 4))
    sin128 = jnp.tile(jnp.concatenate([-sin, sin], axis=1), (1, 2))
    return jnp.repeat(cos128, BATCH, axis=0), jnp.repeat(sin128, BATCH, axis=0)


def _ffn_weights(w_gu, w_down):
    nf = FFN_HIDDEN // FFN_TF
    wg = w_gu[:, :FFN_HIDDEN].astype(BF16).reshape(D_MODEL, nf, FFN_TF).transpose(1, 0, 2)
    wu = w_gu[:, FFN_HIDDEN:].astype(BF16).reshape(D_MODEL, nf, FFN_TF).transpose(1, 0, 2)
    wd = w_down.astype(BF16).reshape(nf, FFN_TF, D_MODEL)
    return wg, wu, wd


def _lru_gate_weights(gate_w, gate_b):
    n_cc = D_MODEL // LRU_CW
    per = LRU_CW // LRU_BW
    w = gate_w.reshape(2, 2, n_cc, per, LRU_BW, LRU_BW)
    eye = jnp.eye(per, dtype=gate_w.dtype)
    bd = jnp.einsum('dgcpij,pq->dgcpiqj', w, eye).reshape(2, 2, n_cc, LRU_CW, LRU_CW)
    wcat = jnp.concatenate([bd[:, 0], bd[:, 1]], axis=-1).astype(BF16)
    b = gate_b.reshape(2, 2, n_cc, 1, LRU_CW)
    bcat = jnp.concatenate([b[:, 0], b[:, 1]], axis=-1)
    return wcat, bcat


def kernel(x, c, ctx, c_ctx, w_mod, b_mod, norm_g, ffn1_w_gu, ffn1_w_down, ffn2_w_gu, ffn2_w_down,
           w_in, ret_decay_logit, w_ret_o, lru_conv_w, lru_conv_b, lru_gate_w, lru_gate_b,
           lru_lambda, w_lru_o, w_out, final_g):
    depth = w_mod.shape[0]
    t_lat, t_ctx = x.shape[1], ctx.shape[1]
    t_all = t_lat + t_ctx
    n_rows = t_all * BATCH
    ctx_rows = t_ctx * BATCH
    assert x.shape[0] == BATCH and x.shape[2] == D_MODEL
    assert t_ctx % LRU_TB == 0 and t_lat % LRU_TB == 0 and ctx_rows % ROW_TILE == 0

    xs = jnp.concatenate([ctx, x], axis=1).transpose(1, 0, 2).reshape(n_rows, D_MODEL)
    tabs = _mod_tables(c, c_ctx, w_mod, b_mod)
    cos_t, sin_t = _rotary_tables(t_ctx, t_lat)
    n_cc = D_MODEL // LRU_CW

    for l in range(depth):
        last = l == depth - 1
        tab = tabs[l]
        g = norm_g[l].reshape(3, 1, D_MODEL)
        xs = _ffn_sublayer(xs, tab, 0, g[0], *_ffn_weights(ffn1_w_gu[l], ffn1_w_down[l]), ctx_rows, 0)

        w_in_c = w_in[l].astype(BF16).reshape(D_MODEL, N_PROJ_CHUNKS, PROJ_CHUNK).transpose(1, 0, 2)
        q, k, v, z = _in_projection(xs, tab, g[1], cos_t, sin_t, w_in_c, ctx_rows)
        logit_rows = jnp.broadcast_to(ret_decay_logit[l].astype(F32)[:, :, None, None],
                                      (2, RET_HEADS, 1, LANES))
        o_seq = _retention(q, k, v, logit_rows, t_ctx // RET_CHUNK)
        gw, gb = _lru_gate_weights(lru_gate_w[l], lru_gate_b[l])
        hf, hb = _rg_lru(z, lru_conv_w[l], lru_conv_b[l].reshape(1, D_MODEL), gw, gb,
                         lru_lambda[l].reshape(2, n_cc, 1, LRU_CW), n_rows, t_ctx // LRU_TB)
        row_start = ctx_rows if last else 0
        xs = _merge(xs, tab, o_seq, hf, hb, z, w_ret_o[l].astype(BF16), w_lru_o[l].astype(BF16),
                    w_out[l].astype(BF16), ctx_rows, row_start)
        xs = _ffn_sublayer(xs, tab, 6, g[2], *_ffn_weights(ffn2_w_gu[l], ffn2_w_down[l]),
                           ctx_rows, row_start)
    return _final_norm(xs, final_g.reshape(1, D_MODEL), t_all, t_ctx)
```

```python
import functools

import jax
import jax.numpy as jnp
from jax import lax
from jax.experimental import pallas as pl
from jax.experimental.pallas import tpu as pltpu

F32 = jnp.float32
BF16 = jnp.bfloat16

D_MODEL = 1024
BATCH = 8
LANES = 128
RET_HEADS = 8
RET_DK = 64
RET_DV = 128
RET_CHUNK = 128
HEAD_GROUP = 4
GRID_W = 64
ROPE_BASE = 10000.0
LRU_BLOCKS = 16
LRU_BW = D_MODEL // LRU_BLOCKS
LRU_C = 8.0
LRU_CW = 256
LRU_TB = 128
CONV_W = 4
FFN_HIDDEN = 2816
FFN_TF = 256
FFN_RES = 0.5
N_MOD = 9
EPS = 1e-6
PROJ_CHUNK = 512
N_PROJ_CHUNKS = 14
ROW_TILE = 512
VMEM_LIMIT = 52 * 1024 * 1024


def _cparams(*sem):
    return pltpu.CompilerParams(dimension_semantics=sem, vmem_limit_bytes=VMEM_LIMIT)


def _resident(shape):
    return pl.BlockSpec(shape, lambda *_: (0,) * len(shape), pipeline_mode=pl.Buffered(1))


def _sigmoid(x):
    return 1.0 / (1.0 + jnp.exp(-x))


def _ada_norm(x, g, shift, scale):
    rows = x.shape[0]
    ms = jnp.mean(x * x, axis=-1, keepdims=True)
    y = (x * lax.rsqrt(ms + EPS)) * g
    y3 = y.reshape(rows // BATCH, BATCH, D_MODEL)
    h = y3 * (1.0 + scale)[None] + shift[None]
    return h.reshape(rows, D_MODEL)


def _mod_kernel(c_ref, w_ref, b_ref, o_ref):
    c = c_ref[...]
    s = (c * _sigmoid(c)).astype(BF16)
    o_ref[...] = jnp.dot(s, w_ref[...].astype(BF16), preferred_element_type=F32) + b_ref[...]


def _mod_tables(c, c_ctx, w_mod, b_mod):
    depth = w_mod.shape[0]
    cc = jnp.zeros((2 * BATCH, D_MODEL), F32).at[:BATCH].set(c).at[BATCH].set(c_ctx)
    out = pl.pallas_call(
        _mod_kernel,
        grid=(depth, N_MOD),
        in_specs=[
            pl.BlockSpec((2 * BATCH, D_MODEL), lambda l, j: (0, 0)),
            pl.BlockSpec((None, D_MODEL, D_MODEL), lambda l, j: (l, 0, j)),
            pl.BlockSpec((None, 1, D_MODEL), lambda l, j: (l, 0, j)),
        ],
        out_specs=pl.BlockSpec((None, 2 * BATCH, D_MODEL), lambda l, j: (l, 0, j)),
        out_shape=jax.ShapeDtypeStruct((depth, 2 * BATCH, N_MOD * D_MODEL), F32),
        compiler_params=_cparams("parallel", "parallel"),
        name="adaln_mod",
    )(cc, w_mod, b_mod.reshape(depth, 1, N_MOD * D_MODEL))
    out = out.reshape(depth, 2 * BATCH, N_MOD, D_MODEL)
    lat = out[:, :BATCH].transpose(0, 2, 1, 3)
    ctx = jnp.broadcast_to(out[:, BATCH][:, :, None, :], lat.shape)
    return jnp.stack([ctx, lat], axis=1)


def _ffn_kernel(x_ref, mod_ref, g_ref, wg_ref, wu_ref, wd_ref, o_ref, h_scr, act_scr, acc_scr):
    rows = x_ref.shape[0]
    nf = wg_ref.shape[0]
    x = x_ref[...]
    h_scr[...] = _ada_norm(x, g_ref[...], mod_ref[0], mod_ref[1]).astype(BF16)
    acc_scr[...] = jnp.zeros_like(acc_scr)

    def gated(k):
        hb = h_scr[...]
        u = jnp.dot(hb, wg_ref[k], preferred_element_type=F32)
        v = jnp.dot(hb, wu_ref[k], preferred_element_type=F32)
        return ((u * _sigmoid(u)) * v).astype(BF16)

    act_scr[...] = gated(0)

    def body(k, carry):
        prev = act_scr[...]
        act_scr[...] = gated(k)
        acc_scr[...] += jnp.dot(prev, wd_ref[k - 1], preferred_element_type=F32)
        return carry

    lax.fori_loop(1, nf, body, 0, unroll=True)
    acc_scr[...] += jnp.dot(act_scr[...], wd_ref[nf - 1], preferred_element_type=F32)
    y3 = acc_scr[...].reshape(rows // BATCH, BATCH, D_MODEL)
    x3 = x.reshape(rows // BATCH, BATCH, D_MODEL)
    o_ref[...] = (x3 + (FFN_RES * mod_ref[2])[None] * y3).reshape(rows, D_MODEL)


def _ffn_sublayer(xs, tab, sub, g, wg, wu, wd, ctx_rows):
    n = xs.shape[0]
    tm = ROW_TILE
    nf = wg.shape[0]
    return pl.pallas_call(
        _ffn_kernel,
        grid=(n // tm,),
        in_specs=[
            pl.BlockSpec((tm, D_MODEL), lambda i: (i, 0)),
            pl.BlockSpec((None, 3, BATCH, D_MODEL),
                         lambda i: ((i * tm >= ctx_rows).astype(jnp.int32), sub // 3, 0, 0)),
            _resident((1, D_MODEL)),
            _resident((nf, D_MODEL, FFN_TF)),
            _resident((nf, D_MODEL, FFN_TF)),
            _resident((nf, FFN_TF, D_MODEL)),
        ],
        out_specs=pl.BlockSpec((tm, D_MODEL), lambda i: (i, 0)),
        out_shape=jax.ShapeDtypeStruct((n, D_MODEL), F32),
        scratch_shapes=[pltpu.VMEM((tm, D_MODEL), BF16), pltpu.VMEM((tm, FFN_TF), BF16),
                        pltpu.VMEM((tm, D_MODEL), F32)],
        compiler_params=_cparams("parallel"),
        name="ffn_sublayer",
    )(xs, tab, g, wg, wu, wd)


def _swap_halves(a):
    w = a.shape[1]
    lane = lax.broadcasted_iota(jnp.int32, a.shape, 1)
    first_half = (lane % RET_DK) < (RET_DK // 2)
    return jnp.where(first_half, pltpu.roll(a, w - RET_DK // 2, 1), pltpu.roll(a, RET_DK // 2, 1))


def _rows_to_sequences(a, slab_scr, out_ref):
    rows, width = a.shape
    steps = rows // BATCH
    n_slabs = width // LANES
    for s in range(n_slabs):
        slab_scr[s] = a[:, s * LANES:(s + 1) * LANES]
    for b in range(BATCH):
        for s in range(n_slabs):
            col = b * width + s * LANES
            out_ref[:, col:col + LANES] = slab_scr[s, pl.ds(b, steps, stride=BATCH), :].astype(out_ref.dtype)


def _inproj_kernel(x_ref, mod_ref, g_ref, cos_ref, sin_ref, w_ref, q_ref, k_ref, v_ref, z_ref,
                   h_scr, slab_scr):
    h_scr[...] = _ada_norm(x_ref[...], g_ref[...], mod_ref[0], mod_ref[1]).astype(BF16)
    reps = PROJ_CHUNK // cos_ref.shape[1]
    cos = jnp.tile(cos_ref[...], (1, reps))
    sin = jnp.tile(sin_ref[...], (1, reps))

    def rotate(a):
        return a * cos + _swap_halves(a) * sin

    q = jnp.dot(h_scr[...], w_ref[0], preferred_element_type=F32)
    _rows_to_sequences(rotate(q), slab_scr, q_ref)
    k = jnp.dot(h_scr[...], w_ref[1], preferred_element_type=F32)
    _rows_to_sequences(rotate(k) * (RET_DK ** -0.5), slab_scr, k_ref)
    for c in range(2):
        v = jnp.dot(h_scr[...], w_ref[2 + c], preferred_element_type=F32)
        _rows_to_sequences(v, slab_scr, v_ref.at[c])

    def body(c, carry):
        z_ref[c] = jnp.dot(h_scr[...], w_ref[c + 4], preferred_element_type=F32).astype(BF16)
        return carry

    lax.fori_loop(0, N_PROJ_CHUNKS - 4, body, 0, unroll=True)


def _in_projection(xs, tab, g, cos_t, sin_t, w_in, ctx_rows):
    n = xs.shape[0]
    tm = ROW_TILE
    nz = N_PROJ_CHUNKS - 4
    steps = tm // BATCH
    t_all = n // BATCH
    seq_w = BATCH * PROJ_CHUNK
    return pl.pallas_call(
        _inproj_kernel,
        grid=(n // tm,),
        in_specs=[
            pl.BlockSpec((tm, D_MODEL), lambda i: (i, 0)),
            pl.BlockSpec((None, 3, BATCH, D_MODEL),
                         lambda i: ((i * tm >= ctx_rows).astype(jnp.int32), 1, 0, 0)),
            _resident((1, D_MODEL)),
            pl.BlockSpec((tm, LANES), lambda i: (i, 0)),
            pl.BlockSpec((tm, LANES), lambda i: (i, 0)),
            _resident((N_PROJ_CHUNKS, D_MODEL, PROJ_CHUNK)),
        ],
        out_specs=[
            pl.BlockSpec((steps, seq_w), lambda i: (i, 0)),
            pl.BlockSpec((steps, seq_w), lambda i: (i, 0)),
            pl.BlockSpec((2, steps, seq_w), lambda i: (0, i, 0)),
            pl.BlockSpec((nz, tm, PROJ_CHUNK), lambda i: (0, i, 0)),
        ],
        out_shape=[
            jax.ShapeDtypeStruct((t_all, seq_w), BF16),
            jax.ShapeDtypeStruct((t_all, seq_w), BF16),
            jax.ShapeDtypeStruct((2, t_all, seq_w), BF16),
            jax.ShapeDtypeStruct((nz, n, PROJ_CHUNK), BF16),
        ],
        scratch_shapes=[pltpu.VMEM((tm, D_MODEL), BF16),
                        pltpu.VMEM((PROJ_CHUNK // LANES, tm, LANES), F32)],
        compiler_params=_cparams("parallel"),
        name="in_projection",
    )(xs, tab, g, cos_t, sin_t, w_in)


def _log_sigmoid(x):
    return jnp.minimum(x, 0.0) - jnp.log1p(jnp.exp(-jnp.abs(x)))


def _dot_t1(a, b):
    return lax.dot_general(a, b, (((1,), (1,)), ((), ())), preferred_element_type=F32)


def _retention_kernel(lg_ref, q_ref, k_ref, v_ref, o_ref,
                      dmat, qdec, kdec, cdec, sb_scr, sf_scr, *, n_ctx_chunks):
    c_len = RET_CHUNK
    pair_k = 2 * RET_DK
    pair_v = 2 * RET_DV
    n_pairs = HEAD_GROUP // 2
    n_chunks = q_ref.shape[0] // c_len
    hg = pl.program_id(1)
    row_i = lax.broadcasted_iota(jnp.int32, (c_len, c_len), 0)
    col_i = lax.broadcasted_iota(jnp.int32, (c_len, c_len), 1)
    row = row_i.astype(F32)
    col = col_i.astype(F32)
    rel = row - col
    first_lanes = col_i < RET_DK
    first_rows = row_i < RET_DK
    own_block = (lax.broadcasted_iota(jnp.int32, (pair_k, pair_v), 0) < RET_DK) == (
        lax.broadcasted_iota(jnp.int32, (pair_k, pair_v), 1) < RET_DV)

    for p in range(n_pairs):
        h0 = hg * HEAD_GROUP + 2 * p
        lg = [[_log_sigmoid(lg_ref[d, h0 + e]) for e in range(2)] for d in range(2)]
        for e in range(2):
            dmat[2 * p + e] = jnp.where(rel >= 0.0, jnp.exp(lg[0][e] * jnp.maximum(rel, 0.0)),
                                        jnp.exp(lg[1][e] * jnp.maximum(-rel, 0.0)))
        lane_f = jnp.where(first_lanes, lg[0][0], lg[0][1])
        lane_b = jnp.where(first_lanes, lg[1][0], lg[1][1])
        row_f = jnp.where(first_rows, lg[0][0], lg[0][1])
        row_b = jnp.where(first_rows, lg[1][0], lg[1][1])
        qdec[0, p] = jnp.exp(lane_f * (row + 1.0))
        qdec[1, p] = jnp.exp(lane_b * (c_len - row))
        kdec[0, p] = jnp.exp(row_f * (c_len - 1.0 - col))
        kdec[1, p] = jnp.exp(row_b * col)
        cdec[0, p] = jnp.exp(row_f * c_len)
        cdec[1, p] = jnp.exp(row_b * c_len)

    def advance(d, p, s, kp, vp):
        kd = (kp.astype(F32).T * kdec[d, p]).astype(BF16)
        upd = jnp.dot(kd, vp, preferred_element_type=F32)
        cd = cdec[d, p]
        return s * jnp.concatenate([cd, cd], axis=1) + jnp.where(own_block, upd, 0.0)

    sf_scr[...] = jnp.zeros_like(sf_scr)

    def bwd_body(i, carry):
        c = jnp.where(i < n_ctx_chunks, n_ctx_chunks - 1 - i, n_chunks - 1 + n_ctx_chunks - i)
        rows = pl.ds(pl.multiple_of(c * c_len, c_len), c_len)
        for p in range(n_pairs):
            kp = k_ref[rows, p * pair_k:(p + 1) * pair_k]
            vp = v_ref[rows, p * pair_v:(p + 1) * pair_v]
            s = sf_scr[p]
            sb_scr[c, p] = s.astype(BF16)
            sf_scr[p] = advance(1, p, s, kp, vp)
        return carry

    lax.fori_loop(0, n_chunks, bwd_body, 0, unroll=3)

    sf_scr[...] = jnp.zeros_like(sf_scr)
    zero_v = jnp.zeros((c_len, RET_DV), BF16)
    zero_q = jnp.zeros((c_len, pair_k), BF16)

    def fwd_body(c, carry):
        rows = pl.ds(pl.multiple_of(c * c_len, c_len), c_len)
        for p in range(n_pairs):
            qp = q_ref[rows, p * pair_k:(p + 1) * pair_k]
            kp = k_ref[rows, p * pair_k:(p + 1) * pair_k]
            vp = v_ref[rows, p * pair_v:(p + 1) * pair_v]
            q_split = jnp.concatenate([jnp.where(first_lanes, qp, zero_q),
                                       jnp.where(first_lanes, zero_q, qp)], axis=0)
            scores = _dot_t1(q_split, kp)
            pa = (scores[:c_len] * dmat[2 * p]).astype(BF16)
            pb = (scores[c_len:] * dmat[2 * p + 1]).astype(BF16)
            qf = qp.astype(F32)
            s = sf_scr[p]
            lhs = jnp.concatenate([pa, pb, (qf * qdec[0, p]).astype(BF16), (qf * qdec[1, p]).astype(BF16)],
                                  axis=1)
            rhs = jnp.concatenate([
                jnp.concatenate([vp[:, :RET_DV], zero_v], axis=1),
                jnp.concatenate([zero_v, vp[:, RET_DV:]], axis=1),
                s.astype(BF16), sb_scr[c, p]], axis=0)
            o = jnp.dot(lhs, rhs, preferred_element_type=F32)
            sf_scr[p] = advance(0, p, s, kp, vp)
            for e in range(2):
                oe = o[:, e * RET_DV:(e + 1) * RET_DV]
                mu = jnp.mean(oe, axis=-1, keepdims=True)
                d = oe - mu
                var = jnp.mean(d * d, axis=-1, keepdims=True)
                col0 = p * pair_v + e * RET_DV
                o_ref[rows, col0:col0 + RET_DV] = (d * lax.rsqrt(var + EPS)).astype(BF16)
        return carry

    lax.fori_loop(0, n_chunks, fwd_body, 0, unroll=3)


def _retention(q, k, v, logit_rows, n_ctx_chunks):
    t_all = q.shape[0]
    n_groups = RET_HEADS // HEAD_GROUP
    n_pairs = HEAD_GROUP // 2
    qw = HEAD_GROUP * RET_DK
    vw = HEAD_GROUP * RET_DV
    assert vw == PROJ_CHUNK and n_groups == v.shape[0]
    n_chunks = t_all // RET_CHUNK
    return pl.pallas_call(
        functools.partial(_retention_kernel, n_ctx_chunks=n_ctx_chunks),
        grid=(BATCH, n_groups),
        in_specs=[
            _resident((2, RET_HEADS, 1, LANES)),
            pl.BlockSpec((t_all, qw), lambda b, h: (0, b * n_groups + h)),
            pl.BlockSpec((t_all, qw), lambda b, h: (0, b * n_groups + h)),
            pl.BlockSpec((None, t_all, vw), lambda b, h: (h, 0, b)),
        ],
        out_specs=pl.BlockSpec((t_all, vw), lambda b, h: (0, b * n_groups + h)),
        out_shape=jax.ShapeDtypeStruct((t_all, BATCH * RET_HEADS * RET_DV), BF16),
        scratch_shapes=[
            pltpu.VMEM((HEAD_GROUP, RET_CHUNK, RET_CHUNK), F32),
            pltpu.VMEM((2, n_pairs, RET_CHUNK, 2 * RET_DK), F32),
            pltpu.VMEM((2, n_pairs, 2 * RET_DK, RET_CHUNK), F32),
            pltpu.VMEM((2, n_pairs, 2 * RET_DK, RET_CHUNK), F32),
            pltpu.VMEM((n_chunks, n_pairs, 2 * RET_DK, 2 * RET_DV), BF16),
            pltpu.VMEM((n_pairs, 2 * RET_DK, 2 * RET_DV), F32),
        ],
        compiler_params=_cparams("parallel", "parallel"),
        name="retention",
    )(logit_rows, q, k, v)


def _lru_kernel(xf_ref, xfp_ref, xfn_ref, xb_ref, xbp_ref, xbn_ref,
                cw_ref, cb_ref, gw_ref, gb_ref, lam_ref, hf_ref, hb_ref,
                a_scr, b_scr, h_scr, *, n_ctx_blocks, n_blocks):
    j = pl.program_id(1)
    jb = jnp.where(j < n_ctx_blocks, n_ctx_blocks - 1 - j, n_blocks - 1 + n_ctx_blocks - j)
    rows = xf_ref.shape[0]
    halo = xfp_ref.shape[0]
    cw = cw_ref[...]
    cb = cb_ref[...]

    def prepare(d, blk, x_ref, xp_ref, xn_ref):
        is_first = jnp.logical_or(blk == 0, blk == n_ctx_blocks)
        is_last = jnp.logical_or(blk == n_ctx_blocks - 1, blk == n_blocks - 1)
        prev = jnp.where(is_first, 0.0, xp_ref[...].astype(F32))
        nxt = jnp.where(is_last, 0.0, xn_ref[...].astype(F32))
        xe = jnp.concatenate([prev, x_ref[...].astype(F32), nxt], axis=0)
        u = cb
        for tap in range(CONV_W):
            off = halo - (2 - tap) * BATCH
            u = u + xe[off:off + rows] * cw[tap:tap + 1]
        t = jnp.tanh(jnp.dot(u.astype(BF16), gw_ref[d], preferred_element_type=F32) + gb_ref[d])
        lam = lam_ref[d]
        softplus = jnp.maximum(-lam, 0.0) + jnp.log1p(jnp.exp(-jnp.abs(lam)))
        half_c = (-0.5 * LRU_C) * softplus
        log_a = half_c * t[:, :LRU_CW] + half_c
        half_u = 0.5 * u
        iu = half_u * t[:, LRU_CW:] + half_u
        th = jnp.tanh(log_a)
        w = (-2.0 * th) / (1.0 - th)
        a_scr[d] = jnp.exp(log_a)
        b_scr[d] = jnp.where(w > 0.0, w * lax.rsqrt(w), 0.0) * iu

    prepare(0, j, xf_ref, xfp_ref, xfn_ref)
    prepare(1, jb, xb_ref, xbp_ref, xbn_ref)

    @pl.when(j == 0)
    def _():
        h_scr[...] = jnp.zeros_like(h_scr)

    n_steps = rows // BATCH

    def step(s, carry):
        hf, hb = carry
        rf = pl.ds(pl.multiple_of(s * BATCH, BATCH), BATCH)
        hf = a_scr[0, rf, :] * hf + b_scr[0, rf, :]
        hf_ref[rf, :] = hf
        rb = pl.ds(pl.multiple_of((n_steps - 1 - s) * BATCH, BATCH), BATCH)
        hb = a_scr[1, rb, :] * hb + b_scr[1, rb, :]
        hb_ref[rb, :] = hb
        return hf, hb

    hf, hb = lax.fori_loop(0, n_steps, step, (h_scr[0], h_scr[1]), unroll=8)
    h_scr[0] = hf
    h_scr[1] = hb


def _rg_lru(z, conv_w, conv_b, gate_w, gate_b, lam, n_rows, n_ctx_blocks):
    rows = LRU_TB * BATCH
    n_blocks = n_rows // rows
    n_cc = D_MODEL // LRU_CW
    per_chunk = PROJ_CHUNK // LRU_CW
    halo = 16
    hb_per_block = rows // halo
    n_halo = n_rows // halo

    def bwd_block(j):
        return jnp.where(j < n_ctx_blocks, n_ctx_blocks - 1 - j, n_blocks - 1 + n_ctx_blocks - j)

    def cur(f):
        return pl.BlockSpec((None, rows, LRU_CW), lambda c, j: (2 + c // per_chunk, f(j), c % per_chunk))

    def prev(f):
        return pl.BlockSpec((None, halo, LRU_CW),
                            lambda c, j: (2 + c // per_chunk, jnp.maximum(f(j) * hb_per_block - 1, 0),
                                          c % per_chunk))

    def nxt(f):
        return pl.BlockSpec((None, halo, LRU_CW),
                            lambda c, j: (2 + c // per_chunk,
                                          jnp.minimum((f(j) + 1) * hb_per_block, n_halo - 1),
                                          c % per_chunk))

    ident = lambda j: j
    return pl.pallas_call(
        functools.partial(_lru_kernel, n_ctx_blocks=n_ctx_blocks, n_blocks=n_blocks),
        grid=(n_cc, n_blocks),
        in_specs=[
            cur(ident), prev(ident), nxt(ident), cur(bwd_block), prev(bwd_block), nxt(bwd_block),
            pl.BlockSpec((CONV_W, LRU_CW), lambda c, j: (0, c)),
            pl.BlockSpec((1, LRU_CW), lambda c, j: (0, c)),
            pl.BlockSpec((2, None, LRU_CW, 2 * LRU_CW), lambda c, j: (0, c, 0, 0)),
            pl.BlockSpec((2, None, 1, 2 * LRU_CW), lambda c, j: (0, c, 0, 0)),
            pl.BlockSpec((2, None, 1, LRU_CW), lambda c, j: (0, c, 0, 0)),
        ],
        out_specs=[
            pl.BlockSpec((rows, LRU_CW), lambda c, j: (j, c)),
            pl.BlockSpec((rows, LRU_CW), lambda c, j: (bwd_block(j), c)),
        ],
        out_shape=[jax.ShapeDtypeStruct((n_rows, D_MODEL), F32)] * 2,
        scratch_shapes=[
            pltpu.VMEM((2, rows, LRU_CW), F32),
            pltpu.VMEM((2, rows, LRU_CW), F32),
            pltpu.VMEM((2, BATCH, LRU_CW), F32),
        ],
        compiler_params=_cparams("parallel", "arbitrary"),
        name="rg_lru",
    )(z, z, z, z, z, z, conv_w, conv_b, gate_w, gate_b, lam)


def _gelu_tanh(x):
    return 0.5 * x * (1.0 + jnp.tanh(0.7978845608028654 * (x + 0.044715 * (x * x * x))))


def _sequences_to_rows(o_ref, slab_scr):
    steps = o_ref.shape[0]
    n_slabs = slab_scr.shape[0]
    width = n_slabs * LANES
    for b in range(BATCH):
        for s in range(n_slabs):
            col = b * width + s * LANES
            slab_scr[s, pl.ds(b, steps, stride=BATCH), :] = o_ref[:, col:col + LANES].astype(F32)


def _merge_kernel(x_ref, mod_ref, o_ref_in, hf_ref, hb_ref, gr0_ref, gr1_ref, gl0_ref, gl1_ref,
                  ga0_ref, ga1_ref, gb0_ref, gb1_ref, wr_ref, wl_ref, wo_ref, o_ref, slab_scr):
    rows = x_ref.shape[0]
    n_slabs = slab_scr.shape[0]
    _sequences_to_rows(o_ref_in, slab_scr)

    def both(r0, r1):
        return jnp.concatenate([r0[...], r1[...]], axis=1).astype(F32)

    gr = both(gr0_ref, gr1_ref)
    o_ret = jnp.concatenate([slab_scr[s] for s in range(n_slabs)], axis=1) * (gr * _sigmoid(gr))
    y_a = jnp.dot(o_ret.astype(BF16), wr_ref[...], preferred_element_type=F32)
    h = hf_ref[...] + hb_ref[...]
    y_b = jnp.dot((h * _gelu_tanh(both(gl0_ref, gl1_ref))).astype(BF16), wl_ref[...],
                  preferred_element_type=F32)
    m = (_sigmoid(both(ga0_ref, ga1_ref)) * y_a + _sigmoid(both(gb0_ref, gb1_ref)) * y_b).astype(BF16)
    y = jnp.dot(m, wo_ref[...], preferred_element_type=F32)
    y3 = y.reshape(rows // BATCH, BATCH, D_MODEL)
    x3 = x_ref[...].reshape(rows // BATCH, BATCH, D_MODEL)
    o_ref[...] = (x3 + mod_ref[2][None] * y3).reshape(rows, D_MODEL)


def _merge(xs, tab, o_seq, hf, hb, z, w_ret_o, w_lru_o, w_out, ctx_rows, row_start):
    n = xs.shape[0]
    tm = ROW_TILE
    t0 = row_start // tm

    def zc(c):
        return pl.BlockSpec((None, tm, PROJ_CHUNK), lambda i: (c, i + t0, 0))

    row = lambda w: pl.BlockSpec((tm, w), lambda i: (i + t0, 0))
    return pl.pallas_call(
        _merge_kernel,
        grid=(n // tm - t0,),
        in_specs=[
            row(D_MODEL),
            pl.BlockSpec((None, 3, BATCH, D_MODEL),
                         lambda i: (((i + t0) * tm >= ctx_rows).astype(jnp.int32), 1, 0, 0)),
            pl.BlockSpec((tm // BATCH, BATCH * D_MODEL), lambda i: (i + t0, 0)),
            row(D_MODEL), row(D_MODEL),
            zc(0), zc(1), zc(4), zc(5), zc(6), zc(7), zc(8), zc(9),
            _resident((D_MODEL, D_MODEL)), _resident((D_MODEL, D_MODEL)), _resident((D_MODEL, D_MODEL)),
        ],
        out_specs=pl.BlockSpec((tm, D_MODEL), lambda i: (i, 0)),
        out_shape=jax.ShapeDtypeStruct((n - row_start, D_MODEL), F32),
        scratch_shapes=[pltpu.VMEM((D_MODEL // LANES, tm, LANES), F32)],
        compiler_params=_cparams("parallel"),
        name="merge_out_proj",
    )(xs, tab, o_seq, hf, hb, z, z, z, z, z, z, z, z, w_ret_o, w_lru_o, w_out)


def _final_kernel(x_ref, g_ref, o_ref):
    x = x_ref[...]
    ms = jnp.mean(x * x, axis=-1, keepdims=True)
    o_ref[...] = (x * lax.rsqrt(ms + EPS)) * g_ref[...]


def _final_norm(xs, g):
    t_lat = xs.shape[0] // BATCH
    tt = max(t for t in (LRU_TB, 2 * LRU_TB, 4 * LRU_TB) if t_lat % t == 0)
    x2 = xs.reshape(t_lat, BATCH * D_MODEL)
    return pl.pallas_call(
        _final_kernel,
        grid=(BATCH, t_lat // tt),
        in_specs=[
            pl.BlockSpec((tt, D_MODEL), lambda b, t: (t, b)),
            pl.BlockSpec((1, D_MODEL), lambda b, t: (0, 0)),
        ],
        out_specs=pl.BlockSpec((None, tt, D_MODEL), lambda b, t: (b, t, 0)),
        out_shape=jax.ShapeDtypeStruct((BATCH, t_lat, D_MODEL), F32),
        compiler_params=_cparams("parallel", "parallel"),
        name="final_norm",
    )(x2, g)


def _rotary_tables(t_ctx, t_lat):
    pos = jnp.arange(t_lat, dtype=jnp.int32)
    row = (pos // GRID_W).astype(F32)
    col = (pos % GRID_W).astype(F32)
    n_f = RET_DK // 4
    inv = ROPE_BASE ** (-jnp.arange(n_f, dtype=F32) / n_f)
    ang = jnp.concatenate([row[:, None] * inv, col[:, None] * inv], axis=-1)
    cos = jnp.concatenate([jnp.ones((t_ctx, RET_DK // 2), F32), jnp.cos(ang)], axis=0)
    sin = jnp.concatenate([jnp.zeros((t_ctx, RET_DK // 2), F32), jnp.sin(ang)], axis=0)
    cos128 = jnp.tile(cos, (1, 4))
    sin128 = jnp.tile(jnp.concatenate([-sin, sin], axis=1), (1, 2))
    return jnp.repeat(cos128, BATCH, axis=0), jnp.repeat(sin128, BATCH, axis=0)


def _ffn_weights(w_gu, w_down):
    nf = FFN_HIDDEN // FFN_TF
    wg = w_gu[:, :FFN_HIDDEN].astype(BF16).reshape(D_MODEL, nf, FFN_TF).transpose(1, 0, 2)
    wu = w_gu[:, FFN_HIDDEN:].astype(BF16).reshape(D_MODEL, nf, FFN_TF).transpose(1, 0, 2)
    wd = w_down.astype(BF16).reshape(nf, FFN_TF, D_MODEL)
    return wg, wu, wd


def _lru_gate_weights(gate_w, gate_b):
    n_cc = D_MODEL // LRU_CW
    per = LRU_CW // LRU_BW
    gate_w = 0.5 * gate_w
    gate_b = 0.5 * gate_b
    w = gate_w.reshape(2, 2, n_cc, per, LRU_BW, LRU_BW)
    eye = jnp.eye(per, dtype=gate_w.dtype)
    bd = jnp.einsum('dgcpij,pq->dgcpiqj', w, eye).reshape(2, 2, n_cc, LRU_CW, LRU_CW)
    wcat = jnp.concatenate([bd[:, 0], bd[:, 1]], axis=-1).astype(BF16)
    b = gate_b.reshape(2, 2, n_cc, 1, LRU_CW)
    bcat = jnp.concatenate([b[:, 0], b[:, 1]], axis=-1)
    return wcat, bcat


def kernel(x, c, ctx, c_ctx, w_mod, b_mod, norm_g, ffn1_w_gu, ffn1_w_down, ffn2_w_gu, ffn2_w_down,
           w_in, ret_decay_logit, w_ret_o, lru_conv_w, lru_conv_b, lru_gate_w, lru_gate_b,
           lru_lambda, w_lru_o, w_out, final_g):
    depth = w_mod.shape[0]
    t_lat, t_ctx = x.shape[1], ctx.shape[1]
    t_all = t_lat + t_ctx
    n_rows = t_all * BATCH
    ctx_rows = t_ctx * BATCH
    assert x.shape[0] == BATCH and x.shape[2] == D_MODEL
    assert t_ctx % LRU_TB == 0 and t_lat % LRU_TB == 0 and ctx_rows % ROW_TILE == 0

    xs = jnp.concatenate([ctx, x], axis=1).transpose(1, 0, 2).reshape(n_rows, D_MODEL)
    tabs = _mod_tables(c, c_ctx, w_mod, b_mod)
    cos_t, sin_t = _rotary_tables(t_ctx, t_lat)
    n_cc = D_MODEL // LRU_CW

    for l in range(depth):
        last = l == depth - 1
        tab = tabs[l]
        g = norm_g[l].reshape(3, 1, D_MODEL)
        xs = _ffn_sublayer(xs, tab, 0, g[0], *_ffn_weights(ffn1_w_gu[l], ffn1_w_down[l]), ctx_rows)

        w_in_c = w_in[l].astype(BF16).reshape(D_MODEL, N_PROJ_CHUNKS, PROJ_CHUNK).transpose(1, 0, 2)
        q, k, v, z = _in_projection(xs, tab, g[1], cos_t, sin_t, w_in_c, ctx_rows)
        logit_rows = jnp.broadcast_to(ret_decay_logit[l].astype(F32)[:, :, None, None],
                                      (2, RET_HEADS, 1, LANES))
        o_seq = _retention(q, k, v, logit_rows, t_ctx // RET_CHUNK)
        gw, gb = _lru_gate_weights(lru_gate_w[l], lru_gate_b[l])
        hf, hb = _rg_lru(z, lru_conv_w[l], lru_conv_b[l].reshape(1, D_MODEL), gw, gb,
                         lru_lambda[l].reshape(2, n_cc, 1, LRU_CW), n_rows, t_ctx // LRU_TB)
        row_start = ctx_rows if last else 0
        xs = _merge(xs, tab, o_seq, hf, hb, z, w_ret_o[l].astype(BF16), w_lru_o[l].astype(BF16),
                    w_out[l].astype(BF16), ctx_rows, row_start)
        xs = _ffn_sublayer(xs, tab, 6, g[2], *_ffn_weights(ffn2_w_gu[l], ffn2_w_down[l]),
                           ctx_rows - row_start)
    return _final_norm(xs, final_g.reshape(1, D_MODEL))
```

```python
import functools

import jax
import jax.numpy as jnp
from jax import lax
from jax.experimental import pallas as pl
from jax.experimental.pallas import tpu as pltpu

F32 = jnp.float32
BF16 = jnp.bfloat16

D_MODEL = 1024
BATCH = 8
LANES = 128
RET_HEADS = 8
RET_DK = 64
RET_DV = 128
RET_CHUNK = 128
HEAD_GROUP = 4
GRID_W = 64
ROPE_BASE = 10000.0
LRU_BLOCKS = 16
LRU_BW = D_MODEL // LRU_BLOCKS
LRU_C = 8.0
LRU_CW = 256
LRU_TB = 128
CONV_W = 4
FFN_HIDDEN = 2816
FFN_TF = 256
FFN_RES = 0.5
N_MOD = 9
EPS = 1e-6
PROJ_CHUNK = 512
N_PROJ_CHUNKS = 14
ROW_TILE = 512
VMEM_LIMIT = 52 * 1024 * 1024


def _cparams(*sem):
    return pltpu.CompilerParams(dimension_semantics=sem, vmem_limit_bytes=VMEM_LIMIT)


def _resident(shape):
    return pl.BlockSpec(shape, lambda *_: (0,) * len(shape), pipeline_mode=pl.Buffered(1))


def _sigmoid(x):
    return 1.0 / (1.0 + jnp.exp(-x))


def _ada_norm(x, g, shift, scale):
    rows = x.shape[0]
    ms = jnp.mean(x * x, axis=-1, keepdims=True)
    y = (x * lax.rsqrt(ms + EPS)) * g
    y3 = y.reshape(rows // BATCH, BATCH, D_MODEL)
    h = y3 * (1.0 + scale)[None] + shift[None]
    return h.reshape(rows, D_MODEL)


def _mod_kernel(c_ref, w_ref, b_ref, o_ref):
    c = c_ref[...]
    s = (c * _sigmoid(c)).astype(BF16)
    o_ref[...] = jnp.dot(s, w_ref[...].astype(BF16), preferred_element_type=F32) + b_ref[...]


def _mod_tables(c, c_ctx, w_mod, b_mod):
    depth = w_mod.shape[0]
    cc = jnp.zeros((2 * BATCH, D_MODEL), F32).at[:BATCH].set(c).at[BATCH].set(c_ctx)
    out = pl.pallas_call(
        _mod_kernel,
        grid=(depth, N_MOD),
        in_specs=[
            pl.BlockSpec((2 * BATCH, D_MODEL), lambda l, j: (0, 0)),
            pl.BlockSpec((None, D_MODEL, D_MODEL), lambda l, j: (l, 0, j)),
            pl.BlockSpec((None, 1, D_MODEL), lambda l, j: (l, 0, j)),
        ],
        out_specs=pl.BlockSpec((None, 2 * BATCH, D_MODEL), lambda l, j: (l, 0, j)),
        out_shape=jax.ShapeDtypeStruct((depth, 2 * BATCH, N_MOD * D_MODEL), F32),
        compiler_params=_cparams("parallel", "parallel"),
        name="adaln_mod",
    )(cc, w_mod, b_mod.reshape(depth, 1, N_MOD * D_MODEL))
    out = out.reshape(depth, 2 * BATCH, N_MOD, D_MODEL)
    lat = out[:, :BATCH].transpose(0, 2, 1, 3)
    ctx = jnp.broadcast_to(out[:, BATCH][:, :, None, :], lat.shape)
    return jnp.stack([ctx, lat], axis=1)


def _ffn_kernel(x_ref, mod_ref, g_ref, wgu_ref, wd_ref, o_ref, h_scr, act_scr, acc_scr):
    rows = x_ref.shape[0]
    nf = FFN_HIDDEN // FFN_TF
    x = x_ref[...]
    h_scr[...] = _ada_norm(x, g_ref[...], mod_ref[0], mod_ref[1]).astype(BF16)
    acc_scr[...] = jnp.zeros_like(acc_scr)

    def gated(k):
        hb = h_scr[...]
        u = jnp.dot(hb, wgu_ref[:, k * FFN_TF:(k + 1) * FFN_TF], preferred_element_type=F32)
        v = jnp.dot(hb, wgu_ref[:, FFN_HIDDEN + k * FFN_TF:FFN_HIDDEN + (k + 1) * FFN_TF],
                    preferred_element_type=F32)
        return ((u * _sigmoid(u)) * v).astype(BF16)

    def down(k):
        return jnp.dot(act_scr[...], wd_ref[k * FFN_TF:(k + 1) * FFN_TF, :], preferred_element_type=F32)

    act_scr[...] = gated(0)
    for k in range(1, nf):
        part = down(k - 1)
        act_scr[...] = gated(k)
        acc_scr[...] += part
    acc_scr[...] += down(nf - 1)
    y3 = acc_scr[...].reshape(rows // BATCH, BATCH, D_MODEL)
    x3 = x.reshape(rows // BATCH, BATCH, D_MODEL)
    o_ref[...] = (x3 + (FFN_RES * mod_ref[2])[None] * y3).reshape(rows, D_MODEL)


def _ffn_sublayer(xs, tab, sub, g, wgu, wd, ctx_rows):
    n = xs.shape[0]
    tm = ROW_TILE
    return pl.pallas_call(
        _ffn_kernel,
        grid=(n // tm,),
        in_specs=[
            pl.BlockSpec((tm, D_MODEL), lambda i: (i, 0)),
            pl.BlockSpec((None, 3, BATCH, D_MODEL),
                         lambda i: ((i * tm >= ctx_rows).astype(jnp.int32), sub // 3, 0, 0)),
            _resident((1, D_MODEL)),
            _resident((D_MODEL, 2 * FFN_HIDDEN)),
            _resident((FFN_HIDDEN, D_MODEL)),
        ],
        out_specs=pl.BlockSpec((tm, D_MODEL), lambda i: (i, 0)),
        out_shape=jax.ShapeDtypeStruct((n, D_MODEL), F32),
        scratch_shapes=[pltpu.VMEM((tm, D_MODEL), BF16), pltpu.VMEM((tm, FFN_TF), BF16),
                        pltpu.VMEM((tm, D_MODEL), F32)],
        compiler_params=_cparams("parallel"),
        name="ffn_sublayer",
    )(xs, tab, g, wgu, wd)


def _swap_halves(a):
    w = a.shape[1]
    lane = lax.broadcasted_iota(jnp.int32, a.shape, 1)
    first_half = (lane % RET_DK) < (RET_DK // 2)
    return jnp.where(first_half, pltpu.roll(a, w - RET_DK // 2, 1), pltpu.roll(a, RET_DK // 2, 1))


def _rows_to_sequences(a, slab_scr, out_ref):
    rows, width = a.shape
    steps = rows // BATCH
    n_slabs = width // LANES
    for s in range(n_slabs):
        slab_scr[s] = a[:, s * LANES:(s + 1) * LANES]
    for b in range(BATCH):
        for s in range(n_slabs):
            col = b * width + s * LANES
            out_ref[:, col:col + LANES] = slab_scr[s, pl.ds(b, steps, stride=BATCH), :].astype(out_ref.dtype)


def _inproj_kernel(x_ref, mod_ref, g_ref, cos_ref, sin_ref, w_ref, q_ref, k_ref, v_ref, z_ref,
                   h_scr, slab_scr):
    h_scr[...] = _ada_norm(x_ref[...], g_ref[...], mod_ref[0], mod_ref[1]).astype(BF16)
    reps = PROJ_CHUNK // cos_ref.shape[1]
    cos = jnp.tile(cos_ref[...], (1, reps))
    sin = jnp.tile(sin_ref[...], (1, reps))

    def rotate(a):
        return a * cos + _swap_halves(a) * sin

    def proj(c):
        return jnp.dot(h_scr[...], w_ref[:, c * PROJ_CHUNK:(c + 1) * PROJ_CHUNK], preferred_element_type=F32)

    _rows_to_sequences(rotate(proj(0)), slab_scr, q_ref)
    _rows_to_sequences(rotate(proj(1)) * (RET_DK ** -0.5), slab_scr, k_ref)
    for c in range(2):
        _rows_to_sequences(proj(2 + c), slab_scr, v_ref.at[c])
    for c in range(4, N_PROJ_CHUNKS):
        z_ref[c - 4] = proj(c).astype(BF16)


def _in_projection(xs, tab, g, cos_t, sin_t, w_in, ctx_rows):
    n = xs.shape[0]
    tm = ROW_TILE
    nz = N_PROJ_CHUNKS - 4
    steps = tm // BATCH
    t_all = n // BATCH
    seq_w = BATCH * PROJ_CHUNK
    return pl.pallas_call(
        _inproj_kernel,
        grid=(n // tm,),
        in_specs=[
            pl.BlockSpec((tm, D_MODEL), lambda i: (i, 0)),
            pl.BlockSpec((None, 3, BATCH, D_MODEL),
                         lambda i: ((i * tm >= ctx_rows).astype(jnp.int32), 1, 0, 0)),
            _resident((1, D_MODEL)),
            pl.BlockSpec((tm, LANES), lambda i: (i, 0)),
            pl.BlockSpec((tm, LANES), lambda i: (i, 0)),
            _resident((D_MODEL, N_PROJ_CHUNKS * PROJ_CHUNK)),
        ],
        out_specs=[
            pl.BlockSpec((steps, seq_w), lambda i: (i, 0)),
            pl.BlockSpec((steps, seq_w), lambda i: (i, 0)),
            pl.BlockSpec((2, steps, seq_w), lambda i: (0, i, 0)),
            pl.BlockSpec((nz, tm, PROJ_CHUNK), lambda i: (0, i, 0)),
        ],
        out_shape=[
            jax.ShapeDtypeStruct((t_all, seq_w), BF16),
            jax.ShapeDtypeStruct((t_all, seq_w), BF16),
            jax.ShapeDtypeStruct((2, t_all, seq_w), BF16),
            jax.ShapeDtypeStruct((nz, n, PROJ_CHUNK), BF16),
        ],
        scratch_shapes=[pltpu.VMEM((tm, D_MODEL), BF16),
                        pltpu.VMEM((PROJ_CHUNK // LANES, tm, LANES), F32)],
        compiler_params=_cparams("parallel"),
        name="in_projection",
    )(xs, tab, g, cos_t, sin_t, w_in)


def _log_sigmoid(x):
    return jnp.minimum(x, 0.0) - jnp.log1p(jnp.exp(-jnp.abs(x)))


def _dot_t1(a, b):
    return lax.dot_general(a, b, (((1,), (1,)), ((), ())), preferred_element_type=F32)


def _retention_kernel(lg_ref, q_ref, k_ref, v_ref, o_ref,
                      dmat, qdec, kdec, cdec, sb_scr, sf_scr, *, n_ctx_chunks):
    c_len = RET_CHUNK
    pair_k = 2 * RET_DK
    pair_v = 2 * RET_DV
    n_pairs = HEAD_GROUP // 2
    n_chunks = q_ref.shape[0] // c_len
    hg = pl.program_id(1)
    row_i = lax.broadcasted_iota(jnp.int32, (c_len, c_len), 0)
    col_i = lax.broadcasted_iota(jnp.int32, (c_len, c_len), 1)
    row = row_i.astype(F32)
    col = col_i.astype(F32)
    rel = row - col
    first_lanes = col_i < RET_DK
    first_rows = row_i < RET_DK
    own_block = (lax.broadcasted_iota(jnp.int32, (pair_k, pair_v), 0) < RET_DK) == (
        lax.broadcasted_iota(jnp.int32, (pair_k, pair_v), 1) < RET_DV)

    for p in range(n_pairs):
        h0 = hg * HEAD_GROUP + 2 * p
        lg = [[_log_sigmoid(lg_ref[d, h0 + e]) for e in range(2)] for d in range(2)]
        for e in range(2):
            dmat[2 * p + e] = jnp.where(rel >= 0.0, jnp.exp(lg[0][e] * jnp.maximum(rel, 0.0)),
                                        jnp.exp(lg[1][e] * jnp.maximum(-rel, 0.0)))
        lane_f = jnp.where(first_lanes, lg[0][0], lg[0][1])
        lane_b = jnp.where(first_lanes, lg[1][0], lg[1][1])
        row_f = jnp.where(first_rows, lg[0][0], lg[0][1])
        row_b = jnp.where(first_rows, lg[1][0], lg[1][1])
        qdec[0, p] = jnp.exp(lane_f * (row + 1.0))
        qdec[1, p] = jnp.exp(lane_b * (c_len - row))
        kdec[0, p] = jnp.exp(row_f * (c_len - 1.0 - col))
        kdec[1, p] = jnp.exp(row_b * col)
        cdec[0, p] = jnp.exp(row_f * c_len)
        cdec[1, p] = jnp.exp(row_b * c_len)

    def advance(d, p, s, kp, vp):
        kd = (kp.astype(F32).T * kdec[d, p]).astype(BF16)
        upd = jnp.dot(kd, vp, preferred_element_type=F32)
        cd = cdec[d, p]
        return s * jnp.concatenate([cd, cd], axis=1) + jnp.where(own_block, upd, 0.0)

    sf_scr[...] = jnp.zeros_like(sf_scr)

    def bwd_body(i, carry):
        c = jnp.where(i < n_ctx_chunks, n_ctx_chunks - 1 - i, n_chunks - 1 + n_ctx_chunks - i)
        rows = pl.ds(pl.multiple_of(c * c_len, c_len), c_len)
        for p in range(n_pairs):
            kp = k_ref[rows, p * pair_k:(p + 1) * pair_k]
            vp = v_ref[rows, p * pair_v:(p + 1) * pair_v]
            s = sf_scr[p]
            sb_scr[c, p] = s.astype(BF16)
            sf_scr[p] = advance(1, p, s, kp, vp)
        return carry

    lax.fori_loop(0, n_chunks, bwd_body, 0, unroll=3)

    sf_scr[...] = jnp.zeros_like(sf_scr)
    zero_v = jnp.zeros((c_len, RET_DV), BF16)
    zero_q = jnp.zeros((c_len, pair_k), BF16)

    def fwd_body(c, carry):
        rows = pl.ds(pl.multiple_of(c * c_len, c_len), c_len)
        for p in range(n_pairs):
            qp = q_ref[rows, p * pair_k:(p + 1) * pair_k]
            kp = k_ref[rows, p * pair_k:(p + 1) * pair_k]
            vp = v_ref[rows, p * pair_v:(p + 1) * pair_v]
            q_split = jnp.concatenate([jnp.where(first_lanes, qp, zero_q),
                                       jnp.where(first_lanes, zero_q, qp)], axis=0)
            scores = _dot_t1(q_split, kp)
            pa = (scores[:c_len] * dmat[2 * p]).astype(BF16)
            pb = (scores[c_len:] * dmat[2 * p + 1]).astype(BF16)
            qf = qp.astype(F32)
            s = sf_scr[p]
            lhs = jnp.concatenate([pa, pb, (qf * qdec[0, p]).astype(BF16), (qf * qdec[1, p]).astype(BF16)],
                                  axis=1)
            rhs = jnp.concatenate([
                jnp.concatenate([vp[:, :RET_DV], zero_v], axis=1),
                jnp.concatenate([zero_v, vp[:, RET_DV:]], axis=1),
                s.astype(BF16), sb_scr[c, p]], axis=0)
            o = jnp.dot(lhs, rhs, preferred_element_type=F32)
            sf_scr[p] = advance(0, p, s, kp, vp)
            for e in range(2):
                oe = o[:, e * RET_DV:(e + 1) * RET_DV]
                mu = jnp.mean(oe, axis=-1, keepdims=True)
                d = oe - mu
                var = jnp.mean(d * d, axis=-1, keepdims=True)
                col0 = p * pair_v + e * RET_DV
                o_ref[rows, col0:col0 + RET_DV] = (d * lax.rsqrt(var + EPS)).astype(BF16)
        return carry

    lax.fori_loop(0, n_chunks, fwd_body, 0, unroll=3)


def _retention(q, k, v, logit_rows, n_ctx_chunks):
    t_all = q.shape[0]
    n_groups = RET_HEADS // HEAD_GROUP
    n_pairs = HEAD_GROUP // 2
    qw = HEAD_GROUP * RET_DK
    vw = HEAD_GROUP * RET_DV
    assert vw == PROJ_CHUNK and n_groups == v.shape[0]
    n_chunks = t_all // RET_CHUNK
    return pl.pallas_call(
        functools.partial(_retention_kernel, n_ctx_chunks=n_ctx_chunks),
        grid=(BATCH, n_groups),
        in_specs=[
            _resident((2, RET_HEADS, 1, LANES)),
            pl.BlockSpec((t_all, qw), lambda b, h: (0, b * n_groups + h)),
            pl.BlockSpec((t_all, qw), lambda b, h: (0, b * n_groups + h)),
            pl.BlockSpec((None, t_all, vw), lambda b, h: (h, 0, b)),
        ],
        out_specs=pl.BlockSpec((t_all, vw), lambda b, h: (0, b * n_groups + h)),
        out_shape=jax.ShapeDtypeStruct((t_all, BATCH * RET_HEADS * RET_DV), BF16),
        scratch_shapes=[
            pltpu.VMEM((HEAD_GROUP, RET_CHUNK, RET_CHUNK), F32),
            pltpu.VMEM((2, n_pairs, RET_CHUNK, 2 * RET_DK), F32),
            pltpu.VMEM((2, n_pairs, 2 * RET_DK, RET_CHUNK), F32),
            pltpu.VMEM((2, n_pairs, 2 * RET_DK, RET_CHUNK), F32),
            pltpu.VMEM((n_chunks, n_pairs, 2 * RET_DK, 2 * RET_DV), BF16),
            pltpu.VMEM((n_pairs, 2 * RET_DK, 2 * RET_DV), F32),
        ],
        compiler_params=_cparams("parallel", "parallel"),
        name="retention",
    )(logit_rows, q, k, v)


def _lru_kernel(xf_ref, xfp_ref, xfn_ref, xb_ref, xbp_ref, xbn_ref,
                cw_ref, cb_ref, gw_ref, gb_ref, lam_ref, hf_ref, hb_ref,
                a_scr, b_scr, h_scr, *, n_ctx_blocks, n_blocks):
    j = pl.program_id(1)
    jb = jnp.where(j < n_ctx_blocks, n_ctx_blocks - 1 - j, n_blocks - 1 + n_ctx_blocks - j)
    rows = xf_ref.shape[0]
    halo = xfp_ref.shape[0]
    cw = cw_ref[...]
    cb = cb_ref[...]

    def prepare(d, blk, x_ref, xp_ref, xn_ref):
        is_first = jnp.logical_or(blk == 0, blk == n_ctx_blocks)
        is_last = jnp.logical_or(blk == n_ctx_blocks - 1, blk == n_blocks - 1)
        prev = jnp.where(is_first, 0.0, xp_ref[...].astype(F32))
        nxt = jnp.where(is_last, 0.0, xn_ref[...].astype(F32))
        xe = jnp.concatenate([prev, x_ref[...].astype(F32), nxt], axis=0)
        u = cb
        for tap in range(CONV_W):
            off = halo - (2 - tap) * BATCH
            u = u + xe[off:off + rows] * cw[tap:tap + 1]
        t = jnp.tanh(jnp.dot(u.astype(BF16), gw_ref[d], preferred_element_type=F32) + gb_ref[d])
        lam = lam_ref[d]
        softplus = jnp.maximum(-lam, 0.0) + jnp.log1p(jnp.exp(-jnp.abs(lam)))
        half_c = (-0.5 * LRU_C) * softplus
        log_a = half_c * t[:, :LRU_CW] + half_c
        half_u = 0.5 * u
        iu = half_u * t[:, LRU_CW:] + half_u
        th = jnp.tanh(log_a)
        w = (-2.0 * th) / (1.0 - th)
        a_scr[d] = jnp.exp(log_a)
        b_scr[d] = jnp.where(w > 0.0, w * lax.rsqrt(w), 0.0) * iu

    prepare(0, j, xf_ref, xfp_ref, xfn_ref)
    prepare(1, jb, xb_ref, xbp_ref, xbn_ref)

    @pl.when(j == 0)
    def _():
        h_scr[...] = jnp.zeros_like(h_scr)

    n_steps = rows // BATCH

    def step(s, carry):
        hf, hb = carry
        rf = pl.ds(pl.multiple_of(s * BATCH, BATCH), BATCH)
        hf = a_scr[0, rf, :] * hf + b_scr[0, rf, :]
        hf_ref[rf, :] = hf
        rb = pl.ds(pl.multiple_of((n_steps - 1 - s) * BATCH, BATCH), BATCH)
        hb = a_scr[1, rb, :] * hb + b_scr[1, rb, :]
        hb_ref[rb, :] = hb
        return hf, hb

    hf, hb = lax.fori_loop(0, n_steps, step, (h_scr[0], h_scr[1]), unroll=8)
    h_scr[0] = hf
    h_scr[1] = hb


def _rg_lru(z, conv_w, conv_b, gate_w, gate_b, lam, n_rows, n_ctx_blocks):
    rows = LRU_TB * BATCH
    n_blocks = n_rows // rows
    n_cc = D_MODEL // LRU_CW
    per_chunk = PROJ_CHUNK // LRU_CW
    halo = 16
    hb_per_block = rows // halo
    n_halo = n_rows // halo

    def bwd_block(j):
        return jnp.where(j < n_ctx_blocks, n_ctx_blocks - 1 - j, n_blocks - 1 + n_ctx_blocks - j)

    def cur(f):
        return pl.BlockSpec((None, rows, LRU_CW), lambda c, j: (2 + c // per_chunk, f(j), c % per_chunk))

    def prev(f):
        return pl.BlockSpec((None, halo, LRU_CW),
                            lambda c, j: (2 + c // per_chunk, jnp.maximum(f(j) * hb_per_block - 1, 0),
                                          c % per_chunk))

    def nxt(f):
        return pl.BlockSpec((None, halo, LRU_CW),
                            lambda c, j: (2 + c // per_chunk,
                                          jnp.minimum((f(j) + 1) * hb_per_block, n_halo - 1),
                                          c % per_chunk))

    ident = lambda j: j
    return pl.pallas_call(
        functools.partial(_lru_kernel, n_ctx_blocks=n_ctx_blocks, n_blocks=n_blocks),
        grid=(n_cc, n_blocks),
        in_specs=[
            cur(ident), prev(ident), nxt(ident), cur(bwd_block), prev(bwd_block), nxt(bwd_block),
            pl.BlockSpec((CONV_W, LRU_CW), lambda c, j: (0, c)),
            pl.BlockSpec((1, LRU_CW), lambda c, j: (0, c)),
            pl.BlockSpec((2, None, LRU_CW, 2 * LRU_CW), lambda c, j: (0, c, 0, 0)),
            pl.BlockSpec((2, None, 1, 2 * LRU_CW), lambda c, j: (0, c, 0, 0)),
            pl.BlockSpec((2, None, 1, LRU_CW), lambda c, j: (0, c, 0, 0)),
        ],
        out_specs=[
            pl.BlockSpec((rows, LRU_CW), lambda c, j: (j, c)),
            pl.BlockSpec((rows, LRU_CW), lambda c, j: (bwd_block(j), c)),
        ],
        out_shape=[jax.ShapeDtypeStruct((n_rows, D_MODEL), F32)] * 2,
        scratch_shapes=[
            pltpu.VMEM((2, rows, LRU_CW), F32),
            pltpu.VMEM((2, rows, LRU_CW), F32),
            pltpu.VMEM((2, BATCH, LRU_CW), F32),
        ],
        compiler_params=_cparams("parallel", "arbitrary"),
        name="rg_lru",
    )(z, z, z, z, z, z, conv_w, conv_b, gate_w, gate_b, lam)


def _gelu_tanh(x):
    return 0.5 * x * (1.0 + jnp.tanh(0.7978845608028654 * (x + 0.044715 * (x * x * x))))


def _sequences_to_rows(o_ref, slab_scr):
    steps = o_ref.shape[0]
    n_slabs = slab_scr.shape[0]
    width = n_slabs * LANES
    for b in range(BATCH):
        for s in range(n_slabs):
            col = b * width + s * LANES
            slab_scr[s, pl.ds(b, steps, stride=BATCH), :] = o_ref[:, col:col + LANES].astype(F32)


def _merge_kernel(x_ref, mod_ref, o_ref_in, hf_ref, hb_ref, gr0_ref, gr1_ref, gl0_ref, gl1_ref,
                  ga0_ref, ga1_ref, gb0_ref, gb1_ref, wr_ref, wl_ref, wo_ref, o_ref, slab_scr):
    rows = x_ref.shape[0]
    n_slabs = slab_scr.shape[0]
    _sequences_to_rows(o_ref_in, slab_scr)

    def both(r0, r1):
        return jnp.concatenate([r0[...], r1[...]], axis=1).astype(F32)

    gr = both(gr0_ref, gr1_ref)
    o_ret = jnp.concatenate([slab_scr[s] for s in range(n_slabs)], axis=1) * (gr * _sigmoid(gr))
    y_a = jnp.dot(o_ret.astype(BF16), wr_ref[...], preferred_element_type=F32)
    h = hf_ref[...] + hb_ref[...]
    y_b = jnp.dot((h * _gelu_tanh(both(gl0_ref, gl1_ref))).astype(BF16), wl_ref[...],
                  preferred_element_type=F32)
    m = (_sigmoid(both(ga0_ref, ga1_ref)) * y_a + _sigmoid(both(gb0_ref, gb1_ref)) * y_b).astype(BF16)
    y = jnp.dot(m, wo_ref[...], preferred_element_type=F32)
    y3 = y.reshape(rows // BATCH, BATCH, D_MODEL)
    x3 = x_ref[...].reshape(rows // BATCH, BATCH, D_MODEL)
    o_ref[...] = (x3 + mod_ref[2][None] * y3).reshape(rows, D_MODEL)


def _merge(xs, tab, o_seq, hf, hb, z, w_ret_o, w_lru_o, w_out, ctx_rows, row_start):
    n = xs.shape[0]
    tm = ROW_TILE
    t0 = row_start // tm

    def zc(c):
        return pl.BlockSpec((None, tm, PROJ_CHUNK), lambda i: (c, i + t0, 0))

    row = lambda w: pl.BlockSpec((tm, w), lambda i: (i + t0, 0))
    return pl.pallas_call(
        _merge_kernel,
        grid=(n // tm - t0,),
        in_specs=[
            row(D_MODEL),
            pl.BlockSpec((None, 3, BATCH, D_MODEL),
                         lambda i: (((i + t0) * tm >= ctx_rows).astype(jnp.int32), 1, 0, 0)),
            pl.BlockSpec((tm // BATCH, BATCH * D_MODEL), lambda i: (i + t0, 0)),
            row(D_MODEL), row(D_MODEL),
            zc(0), zc(1), zc(4), zc(5), zc(6), zc(7), zc(8), zc(9),
            _resident((D_MODEL, D_MODEL)), _resident((D_MODEL, D_MODEL)), _resident((D_MODEL, D_MODEL)),
        ],
        out_specs=pl.BlockSpec((tm, D_MODEL), lambda i: (i, 0)),
        out_shape=jax.ShapeDtypeStruct((n - row_start, D_MODEL), F32),
        scratch_shapes=[pltpu.VMEM((D_MODEL // LANES, tm, LANES), F32)],
        compiler_params=_cparams("parallel"),
        name="merge_out_proj",
    )(xs, tab, o_seq, hf, hb, z, z, z, z, z, z, z, z, w_ret_o, w_lru_o, w_out)


def _final_kernel(x_ref, g_ref, o_ref, slab_scr):
    x = x_ref[...]
    steps = x.shape[0] // BATCH
    ms = jnp.mean(x * x, axis=-1, keepdims=True)
    y = (x * lax.rsqrt(ms + EPS)) * g_ref[...]
    n_slabs = slab_scr.shape[0]
    for s in range(n_slabs):
        slab_scr[s] = y[:, s * LANES:(s + 1) * LANES]
    for b in range(BATCH):
        for s in range(n_slabs):
            o_ref[b, :, s * LANES:(s + 1) * LANES] = slab_scr[s, pl.ds(b, steps, stride=BATCH), :]


def _final_norm(xs, g):
    n = xs.shape[0]
    tm = ROW_TILE
    return pl.pallas_call(
        _final_kernel,
        grid=(n // tm,),
        in_specs=[
            pl.BlockSpec((tm, D_MODEL), lambda i: (i, 0)),
            pl.BlockSpec((1, D_MODEL), lambda i: (0, 0)),
        ],
        out_specs=pl.BlockSpec((BATCH, tm // BATCH, D_MODEL), lambda i: (0, i, 0)),
        out_shape=jax.ShapeDtypeStruct((BATCH, n // BATCH, D_MODEL), F32),
        scratch_shapes=[pltpu.VMEM((D_MODEL // LANES, tm, LANES), F32)],
        compiler_params=_cparams("parallel"),
        name="final_norm",
    )(xs, g)


def _time_major_kernel(ctx_ref, x_ref, o_ref, slab_scr, *, n_ctx_tiles):
    steps = o_ref.shape[0] // BATCH
    n_slabs = slab_scr.shape[0]

    def emit(src_ref):
        for b in range(BATCH):
            for s in range(n_slabs):
                slab_scr[s, pl.ds(b, steps, stride=BATCH), :] = src_ref[b, :, s * LANES:(s + 1) * LANES]
        o_ref[...] = jnp.concatenate([slab_scr[s] for s in range(n_slabs)], axis=1)

    @pl.when(pl.program_id(0) < n_ctx_tiles)
    def _():
        emit(ctx_ref)

    @pl.when(pl.program_id(0) >= n_ctx_tiles)
    def _():
        emit(x_ref)


def _to_time_major(ctx, x):
    t_ctx, t_lat = ctx.shape[1], x.shape[1]
    tm = ROW_TILE
    steps = tm // BATCH
    n_ctx_tiles = t_ctx // steps
    n = (t_ctx + t_lat) * BATCH
    return pl.pallas_call(
        functools.partial(_time_major_kernel, n_ctx_tiles=n_ctx_tiles),
        grid=(n // tm,),
        in_specs=[
            pl.BlockSpec((BATCH, steps, D_MODEL), lambda i: (0, jnp.minimum(i, n_ctx_tiles - 1), 0)),
            pl.BlockSpec((BATCH, steps, D_MODEL), lambda i: (0, jnp.maximum(i - n_ctx_tiles, 0), 0)),
        ],
        out_specs=pl.BlockSpec((tm, D_MODEL), lambda i: (i, 0)),
        out_shape=jax.ShapeDtypeStruct((n, D_MODEL), F32),
        scratch_shapes=[pltpu.VMEM((D_MODEL // LANES, tm, LANES), F32)],
        compiler_params=_cparams("parallel"),
        name="to_time_major",
    )(ctx, x)


def _rotary_tables(t_ctx, t_lat):
    pos = jnp.arange(t_lat, dtype=jnp.int32)
    row = (pos // GRID_W).astype(F32)
    col = (pos % GRID_W).astype(F32)
    n_f = RET_DK // 4
    inv = ROPE_BASE ** (-jnp.arange(n_f, dtype=F32) / n_f)
    ang = jnp.concatenate([row[:, None] * inv, col[:, None] * inv], axis=-1)
    cos = jnp.concatenate([jnp.ones((t_ctx, RET_DK // 2), F32), jnp.cos(ang)], axis=0)
    sin = jnp.concatenate([jnp.zeros((t_ctx, RET_DK // 2), F32), jnp.sin(ang)], axis=0)
    cos128 = jnp.tile(cos, (1, 4))
    sin128 = jnp.tile(jnp.concatenate([-sin, sin], axis=1), (1, 2))
    return jnp.repeat(cos128, BATCH, axis=0), jnp.repeat(sin128, BATCH, axis=0)


def _lru_gate_weights(gate_w, gate_b):
    n_cc = D_MODEL // LRU_CW
    per = LRU_CW // LRU_BW
    gate_w = 0.5 * gate_w
    gate_b = 0.5 * gate_b
    w = gate_w.reshape(2, 2, n_cc, per, LRU_BW, LRU_BW)
    eye = jnp.eye(per, dtype=gate_w.dtype)
    bd = jnp.einsum('dgcpij,pq->dgcpiqj', w, eye).reshape(2, 2, n_cc, LRU_CW, LRU_CW)
    wcat = jnp.concatenate([bd[:, 0], bd[:, 1]], axis=-1).astype(BF16)
    b = gate_b.reshape(2, 2, n_cc, 1, LRU_CW)
    bcat = jnp.concatenate([b[:, 0], b[:, 1]], axis=-1)
    return wcat, bcat


def kernel(x, c, ctx, c_ctx, w_mod, b_mod, norm_g, ffn1_w_gu, ffn1_w_down, ffn2_w_gu, ffn2_w_down,
           w_in, ret_decay_logit, w_ret_o, lru_conv_w, lru_conv_b, lru_gate_w, lru_gate_b,
           lru_lambda, w_lru_o, w_out, final_g):
    depth = w_mod.shape[0]
    t_lat, t_ctx = x.shape[1], ctx.shape[1]
    t_all = t_lat + t_ctx
    n_rows = t_all * BATCH
    ctx_rows = t_ctx * BATCH
    assert x.shape[0] == BATCH and x.shape[2] == D_MODEL
    assert t_ctx % LRU_TB == 0 and t_lat % LRU_TB == 0 and ctx_rows % ROW_TILE == 0

    xs = _to_time_major(ctx, x)
    tabs = _mod_tables(c, c_ctx, w_mod, b_mod)
    cos_t, sin_t = _rotary_tables(t_ctx, t_lat)
    n_cc = D_MODEL // LRU_CW

    for l in range(depth):
        last = l == depth - 1
        tab = tabs[l]
        g = norm_g[l].reshape(3, 1, D_MODEL)
        xs = _ffn_sublayer(xs, tab, 0, g[0], ffn1_w_gu[l].astype(BF16), ffn1_w_down[l].astype(BF16), ctx_rows)
        q, k, v, z = _in_projection(xs, tab, g[1], cos_t, sin_t, w_in[l].astype(BF16), ctx_rows)
        logit_rows = jnp.broadcast_to(ret_decay_logit[l].astype(F32)[:, :, None, None],
                                      (2, RET_HEADS, 1, LANES))
        o_seq = _retention(q, k, v, logit_rows, t_ctx // RET_CHUNK)
        gw, gb = _lru_gate_weights(lru_gate_w[l], lru_gate_b[l])
        hf, hb = _rg_lru(z, lru_conv_w[l], lru_conv_b[l].reshape(1, D_MODEL), gw, gb,
                         lru_lambda[l].reshape(2, n_cc, 1, LRU_CW), n_rows, t_ctx // LRU_TB)
        row_start = ctx_rows if last else 0
        xs = _merge(xs, tab, o_seq, hf, hb, z, w_ret_o[l].astype(BF16), w_lru_o[l].astype(BF16),
                    w_out[l].astype(BF16), ctx_rows, row_start)
        xs = _ffn_sublayer(xs, tab, 6, g[2], ffn2_w_gu[l].astype(BF16), ffn2_w_down[l].astype(BF16),
                           ctx_rows - row_start)
    return _final_norm(xs, final_g.reshape(1, D_MODEL))
```

```python
import functools

import jax
import jax.numpy as jnp
from jax import lax
from jax.experimental import pallas as pl
from jax.experimental.pallas import tpu as pltpu

F32 = jnp.float32
BF16 = jnp.bfloat16

D_MODEL = 1024
BATCH = 8
LANES = 128
RET_HEADS = 8
RET_DK = 64
RET_DV = 128
RET_CHUNK = 128
HEAD_GROUP = 4
GRID_W = 64
ROPE_BASE = 10000.0
LRU_BLOCKS = 16
LRU_BW = D_MODEL // LRU_BLOCKS
LRU_C = 8.0
LRU_CW = 256
LRU_TB = 128
CONV_W = 4
FFN_HIDDEN = 2816
FFN_TF = 256
FFN_RES = 0.5
N_MOD = 9
EPS = 1e-6
PROJ_CHUNK = 512
N_PROJ_CHUNKS = 14
ROW_TILE = 512
VMEM_LIMIT = 52 * 1024 * 1024


def _cparams(*sem):
    return pltpu.CompilerParams(dimension_semantics=sem, vmem_limit_bytes=VMEM_LIMIT)


def _resident(shape):
    return pl.BlockSpec(shape, lambda *_: (0,) * len(shape), pipeline_mode=pl.Buffered(1))


def _sigmoid(x):
    return 1.0 / (1.0 + jnp.exp(-x))


def _ada_norm(x, g, shift, scale):
    rows = x.shape[0]
    ms = jnp.mean(x * x, axis=-1, keepdims=True)
    y = (x * lax.rsqrt(ms + EPS)) * g
    y3 = y.reshape(rows // BATCH, BATCH, D_MODEL)
    h = y3 * (1.0 + scale)[None] + shift[None]
    return h.reshape(rows, D_MODEL)


def _mod_kernel(c_ref, w_ref, b_ref, o_ref):
    c = c_ref[...]
    s = (c * _sigmoid(c)).astype(BF16)
    o_ref[...] = jnp.dot(s, w_ref[...].astype(BF16), preferred_element_type=F32) + b_ref[...]


def _mod_tables(c, c_ctx, w_mod, b_mod):
    depth = w_mod.shape[0]
    cc = jnp.zeros((2 * BATCH, D_MODEL), F32).at[:BATCH].set(c).at[BATCH].set(c_ctx)
    out = pl.pallas_call(
        _mod_kernel,
        grid=(depth, N_MOD),
        in_specs=[
            pl.BlockSpec((2 * BATCH, D_MODEL), lambda l, j: (0, 0)),
            pl.BlockSpec((None, D_MODEL, D_MODEL), lambda l, j: (l, 0, j)),
            pl.BlockSpec((None, 1, D_MODEL), lambda l, j: (l, 0, j)),
        ],
        out_specs=pl.BlockSpec((None, 2 * BATCH, D_MODEL), lambda l, j: (l, 0, j)),
        out_shape=jax.ShapeDtypeStruct((depth, 2 * BATCH, N_MOD * D_MODEL), F32),
        compiler_params=_cparams("parallel", "parallel"),
        name="adaln_mod",
    )(cc, w_mod, b_mod.reshape(depth, 1, N_MOD * D_MODEL))
    out = out.reshape(depth, 2 * BATCH, N_MOD, D_MODEL)
    lat = out[:, :BATCH].transpose(0, 2, 1, 3)
    ctx = jnp.broadcast_to(out[:, BATCH][:, :, None, :], lat.shape)
    return jnp.stack([ctx, lat], axis=1)


def _ffn_kernel(x_ref, mod_ref, g_ref, wg32_ref, wu32_ref, wd32_ref, o_ref,
                wg_s, wu_s, wd_s, h_scr, act_scr, acc_scr):
    nf = FFN_HIDDEN // FFN_TF
    step = pl.program_id(0)

    @pl.when(step < nf)
    def _():
        wg_s[step] = wg32_ref[...].astype(BF16)
        wu_s[step] = wu32_ref[...].astype(BF16)
        wd_s[step] = wd32_ref[...].astype(BF16)

    @pl.when(step >= nf)
    def _():
        rows = x_ref.shape[0]
        x = x_ref[...]
        h_scr[...] = _ada_norm(x, g_ref[...], mod_ref[0], mod_ref[1]).astype(BF16)
        acc_scr[...] = jnp.zeros_like(acc_scr)

        def gated(k):
            hb = h_scr[...]
            u = jnp.dot(hb, wg_s[k], preferred_element_type=F32)
            v = jnp.dot(hb, wu_s[k], preferred_element_type=F32)
            return ((u * _sigmoid(u)) * v).astype(BF16)

        def down(k):
            return jnp.dot(act_scr[...], wd_s[k], preferred_element_type=F32)

        act_scr[...] = gated(0)
        for k in range(1, nf):
            part = down(k - 1)
            act_scr[...] = gated(k)
            acc_scr[...] += part
        acc_scr[...] += down(nf - 1)
        y3 = acc_scr[...].reshape(rows // BATCH, BATCH, D_MODEL)
        x3 = x.reshape(rows // BATCH, BATCH, D_MODEL)
        o_ref[...] = (x3 + (FFN_RES * mod_ref[2])[None] * y3).reshape(rows, D_MODEL)


def _ffn_sublayer(xs, tab, sub, g, w_gu, w_down, layer, ctx_rows):
    n = xs.shape[0]
    tm = ROW_TILE
    nf = FFN_HIDDEN // FFN_TF
    tile = lambda i: jnp.maximum(i - nf, 0)
    chunk = lambda i: jnp.minimum(i, nf - 1)
    return pl.pallas_call(
        _ffn_kernel,
        grid=(nf + n // tm,),
        in_specs=[
            pl.BlockSpec((tm, D_MODEL), lambda i: (tile(i), 0)),
            pl.BlockSpec((None, 3, BATCH, D_MODEL),
                         lambda i: ((tile(i) * tm >= ctx_rows).astype(jnp.int32), sub // 3, 0, 0)),
            _resident((1, D_MODEL)),
            pl.BlockSpec((None, D_MODEL, FFN_TF), lambda i: (layer, 0, chunk(i))),
            pl.BlockSpec((None, D_MODEL, FFN_TF), lambda i: (layer, 0, nf + chunk(i))),
            pl.BlockSpec((None, FFN_TF, D_MODEL), lambda i: (layer, chunk(i), 0)),
        ],
        out_specs=pl.BlockSpec((tm, D_MODEL), lambda i: (tile(i), 0)),
        out_shape=jax.ShapeDtypeStruct((n, D_MODEL), F32),
        scratch_shapes=[pltpu.VMEM((nf, D_MODEL, FFN_TF), BF16), pltpu.VMEM((nf, D_MODEL, FFN_TF), BF16),
                        pltpu.VMEM((nf, FFN_TF, D_MODEL), BF16),
                        pltpu.VMEM((tm, D_MODEL), BF16), pltpu.VMEM((tm, FFN_TF), BF16),
                        pltpu.VMEM((tm, D_MODEL), F32)],
        compiler_params=_cparams("arbitrary"),
        name="ffn_sublayer",
    )(xs, tab, g, w_gu, w_gu, w_down)


def _swap_halves(a):
    w = a.shape[1]
    lane = lax.broadcasted_iota(jnp.int32, a.shape, 1)
    first_half = (lane % RET_DK) < (RET_DK // 2)
    return jnp.where(first_half, pltpu.roll(a, w - RET_DK // 2, 1), pltpu.roll(a, RET_DK // 2, 1))


def _rows_to_sequences(a, slab_scr, out_ref):
    rows, width = a.shape
    steps = rows // BATCH
    n_slabs = width // LANES
    for s in range(n_slabs):
        slab_scr[s] = a[:, s * LANES:(s + 1) * LANES]
    for b in range(BATCH):
        for s in range(n_slabs):
            col = b * width + s * LANES
            out_ref[:, col:col + LANES] = slab_scr[s, pl.ds(b, steps, stride=BATCH), :].astype(out_ref.dtype)


def _inproj_kernel(x_ref, mod_ref, g_ref, cos_ref, sin_ref, w32_ref, q_ref, k_ref, v_ref, z_ref,
                   w_s, h_scr, slab_scr):
    step = pl.program_id(0)

    @pl.when(step < N_PROJ_CHUNKS)
    def _():
        w_s[step] = w32_ref[...].astype(BF16)

    @pl.when(step >= N_PROJ_CHUNKS)
    def _():
        h_scr[...] = _ada_norm(x_ref[...], g_ref[...], mod_ref[0], mod_ref[1]).astype(BF16)
        reps = PROJ_CHUNK // cos_ref.shape[1]
        cos = jnp.tile(cos_ref[...], (1, reps))
        sin = jnp.tile(sin_ref[...], (1, reps))

        def rotate(a):
            return a * cos + _swap_halves(a) * sin

        def proj(c):
            return jnp.dot(h_scr[...], w_s[c], preferred_element_type=F32)

        _rows_to_sequences(rotate(proj(0)), slab_scr, q_ref)
        _rows_to_sequences(rotate(proj(1)) * (RET_DK ** -0.5), slab_scr, k_ref)
        for c in range(2):
            _rows_to_sequences(proj(2 + c), slab_scr, v_ref.at[c])
        for c in range(4, N_PROJ_CHUNKS):
            z_ref[c - 4] = proj(c).astype(BF16)


def _in_projection(xs, tab, g, cos_t, sin_t, w_in, layer, ctx_rows):
    n = xs.shape[0]
    tm = ROW_TILE
    nz = N_PROJ_CHUNKS - 4
    steps = tm // BATCH
    t_all = n // BATCH
    seq_w = BATCH * PROJ_CHUNK
    nw = N_PROJ_CHUNKS
    tile = lambda i: jnp.maximum(i - nw, 0)
    return pl.pallas_call(
        _inproj_kernel,
        grid=(nw + n // tm,),
        in_specs=[
            pl.BlockSpec((tm, D_MODEL), lambda i: (tile(i), 0)),
            pl.BlockSpec((None, 3, BATCH, D_MODEL),
                         lambda i: ((tile(i) * tm >= ctx_rows).astype(jnp.int32), 1, 0, 0)),
            _resident((1, D_MODEL)),
            pl.BlockSpec((tm, LANES), lambda i: (tile(i), 0)),
            pl.BlockSpec((tm, LANES), lambda i: (tile(i), 0)),
            pl.BlockSpec((None, D_MODEL, PROJ_CHUNK), lambda i: (layer, 0, jnp.minimum(i, nw - 1))),
        ],
        out_specs=[
            pl.BlockSpec((steps, seq_w), lambda i: (tile(i), 0)),
            pl.BlockSpec((steps, seq_w), lambda i: (tile(i), 0)),
            pl.BlockSpec((2, steps, seq_w), lambda i: (0, tile(i), 0)),
            pl.BlockSpec((nz, tm, PROJ_CHUNK), lambda i: (0, tile(i), 0)),
        ],
        out_shape=[
            jax.ShapeDtypeStruct((t_all, seq_w), BF16),
            jax.ShapeDtypeStruct((t_all, seq_w), BF16),
            jax.ShapeDtypeStruct((2, t_all, seq_w), BF16),
            jax.ShapeDtypeStruct((nz, n, PROJ_CHUNK), BF16),
        ],
        scratch_shapes=[pltpu.VMEM((N_PROJ_CHUNKS, D_MODEL, PROJ_CHUNK), BF16),
                        pltpu.VMEM((tm, D_MODEL), BF16),
                        pltpu.VMEM((PROJ_CHUNK // LANES, tm, LANES), F32)],
        compiler_params=_cparams("arbitrary"),
        name="in_projection",
    )(xs, tab, g, cos_t, sin_t, w_in)


def _log_sigmoid(x):
    return jnp.minimum(x, 0.0) - jnp.log1p(jnp.exp(-jnp.abs(x)))


def _dot_t1(a, b):
    return lax.dot_general(a, b, (((1,), (1,)), ((), ())), preferred_element_type=F32)


def _retention_kernel(lg_ref, q_ref, k_ref, v_ref, o_ref,
                      dmat, qdec, kdec, cdec, sb_scr, sf_scr, *, n_ctx_chunks):
    c_len = RET_CHUNK
    pair_k = 2 * RET_DK
    pair_v = 2 * RET_DV
    n_pairs = HEAD_GROUP // 2
    n_chunks = q_ref.shape[0] // c_len
    hg = pl.program_id(1)
    row_i = lax.broadcasted_iota(jnp.int32, (c_len, c_len), 0)
    col_i = lax.broadcasted_iota(jnp.int32, (c_len, c_len), 1)
    row = row_i.astype(F32)
    col = col_i.astype(F32)
    rel = row - col
    first_lanes = col_i < RET_DK
    first_rows = row_i < RET_DK
    own_block = (lax.broadcasted_iota(jnp.int32, (pair_k, pair_v), 0) < RET_DK) == (
        lax.broadcasted_iota(jnp.int32, (pair_k, pair_v), 1) < RET_DV)

    for p in range(n_pairs):
        h0 = hg * HEAD_GROUP + 2 * p
        lg = [[_log_sigmoid(lg_ref[d, h0 + e]) for e in range(2)] for d in range(2)]
        for e in range(2):
            dmat[2 * p + e] = jnp.where(rel >= 0.0, jnp.exp(lg[0][e] * jnp.maximum(rel, 0.0)),
                                        jnp.exp(lg[1][e] * jnp.maximum(-rel, 0.0)))
        lane_f = jnp.where(first_lanes, lg[0][0], lg[0][1])
        lane_b = jnp.where(first_lanes, lg[1][0], lg[1][1])
        row_f = jnp.where(first_rows, lg[0][0], lg[0][1])
        row_b = jnp.where(first_rows, lg[1][0], lg[1][1])
        qdec[0, p] = jnp.exp(lane_f * (row + 1.0))
        qdec[1, p] = jnp.exp(lane_b * (c_len - row))
        kdec[0, p] = jnp.exp(row_f * (c_len - 1.0 - col))
        kdec[1, p] = jnp.exp(row_b * col)
        cdec[0, p] = jnp.exp(row_f * c_len)
        cdec[1, p] = jnp.exp(row_b * c_len)

    def advance(d, p, s, kp, vp):
        kd = (kp.astype(F32).T * kdec[d, p]).astype(BF16)
        upd = jnp.dot(kd, vp, preferred_element_type=F32)
        cd = cdec[d, p]
        return s * jnp.concatenate([cd, cd], axis=1) + jnp.where(own_block, upd, 0.0)

    sf_scr[...] = jnp.zeros_like(sf_scr)

    def bwd_body(i, carry):
        c = jnp.where(i < n_ctx_chunks, n_ctx_chunks - 1 - i, n_chunks - 1 + n_ctx_chunks - i)
        rows = pl.ds(pl.multiple_of(c * c_len, c_len), c_len)
        for p in range(n_pairs):
            kp = k_ref[rows, p * pair_k:(p + 1) * pair_k]
            vp = v_ref[rows, p * pair_v:(p + 1) * pair_v]
            s = sf_scr[p]
            sb_scr[c, p] = s.astype(BF16)
            sf_scr[p] = advance(1, p, s, kp, vp)
        return carry

    lax.fori_loop(0, n_chunks, bwd_body, 0, unroll=3)

    sf_scr[...] = jnp.zeros_like(sf_scr)
    zero_v = jnp.zeros((c_len, RET_DV), BF16)
    zero_q = jnp.zeros((c_len, pair_k), BF16)

    def fwd_body(c, carry):
        rows = pl.ds(pl.multiple_of(c * c_len, c_len), c_len)
        for p in range(n_pairs):
            qp = q_ref[rows, p * pair_k:(p + 1) * pair_k]
            kp = k_ref[rows, p * pair_k:(p + 1) * pair_k]
            vp = v_ref[rows, p * pair_v:(p + 1) * pair_v]
            q_split = jnp.concatenate([jnp.where(first_lanes, qp, zero_q),
                                       jnp.where(first_lanes, zero_q, qp)], axis=0)
            scores = _dot_t1(q_split, kp)
            pa = (scores[:c_len] * dmat[2 * p]).astype(BF16)
            pb = (scores[c_len:] * dmat[2 * p + 1]).astype(BF16)
            qf = qp.astype(F32)
            s = sf_scr[p]
            lhs = jnp.concatenate([pa, pb, (qf * qdec[0, p]).astype(BF16), (qf * qdec[1, p]).astype(BF16)],
                                  axis=1)
            rhs = jnp.concatenate([
                jnp.concatenate([vp[:, :RET_DV], zero_v], axis=1),
                jnp.concatenate([zero_v, vp[:, RET_DV:]], axis=1),
                s.astype(BF16), sb_scr[c, p]], axis=0)
            o = jnp.dot(lhs, rhs, preferred_element_type=F32)
            sf_scr[p] = advance(0, p, s, kp, vp)
            for e in range(2):
                oe = o[:, e * RET_DV:(e + 1) * RET_DV]
                mu = jnp.mean(oe, axis=-1, keepdims=True)
                d = oe - mu
                var = jnp.mean(d * d, axis=-1, keepdims=True)
                col0 = p * pair_v + e * RET_DV
                o_ref[rows, col0:col0 + RET_DV] = (d * lax.rsqrt(var + EPS)).astype(BF16)
        return carry

    lax.fori_loop(0, n_chunks, fwd_body, 0, unroll=3)


def _retention(q, k, v, logit_rows, n_ctx_chunks):
    t_all = q.shape[0]
    n_groups = RET_HEADS // HEAD_GROUP
    n_pairs = HEAD_GROUP // 2
    qw = HEAD_GROUP * RET_DK
    vw = HEAD_GROUP * RET_DV
    assert vw == PROJ_CHUNK and n_groups == v.shape[0]
    n_chunks = t_all // RET_CHUNK
    return pl.pallas_call(
        functools.partial(_retention_kernel, n_ctx_chunks=n_ctx_chunks),
        grid=(BATCH, n_groups),
        in_specs=[
            _resident((2, RET_HEADS, 1, LANES)),
            pl.BlockSpec((t_all, qw), lambda b, h: (0, b * n_groups + h)),
            pl.BlockSpec((t_all, qw), lambda b, h: (0, b * n_groups + h)),
            pl.BlockSpec((None, t_all, vw), lambda b, h: (h, 0, b)),
        ],
        out_specs=pl.BlockSpec((t_all, vw), lambda b, h: (0, b * n_groups + h)),
        out_shape=jax.ShapeDtypeStruct((t_all, BATCH * RET_HEADS * RET_DV), BF16),
        scratch_shapes=[
            pltpu.VMEM((HEAD_GROUP, RET_CHUNK, RET_CHUNK), F32),
            pltpu.VMEM((2, n_pairs, RET_CHUNK, 2 * RET_DK), F32),
            pltpu.VMEM((2, n_pairs, 2 * RET_DK, RET_CHUNK), F32),
            pltpu.VMEM((2, n_pairs, 2 * RET_DK, RET_CHUNK), F32),
            pltpu.VMEM((n_chunks, n_pairs, 2 * RET_DK, 2 * RET_DV), BF16),
            pltpu.VMEM((n_pairs, 2 * RET_DK, 2 * RET_DV), F32),
        ],
        compiler_params=_cparams("parallel", "parallel"),
        name="retention",
    )(logit_rows, q, k, v)


def _lru_kernel(xf_ref, xfp_ref, xfn_ref, xb_ref, xbp_ref, xbn_ref,
                cw_ref, cb_ref, gw_ref, gb_ref, lam_ref, hf_ref, hb_ref,
                a_scr, b_scr, h_scr, *, n_ctx_blocks, n_blocks):
    j = pl.program_id(1)
    jb = jnp.where(j < n_ctx_blocks, n_ctx_blocks - 1 - j, n_blocks - 1 + n_ctx_blocks - j)
    rows = xf_ref.shape[0]
    halo = xfp_ref.shape[0]
    cw = cw_ref[...]
    cb = cb_ref[...]

    def prepare(d, blk, x_ref, xp_ref, xn_ref):
        is_first = jnp.logical_or(blk == 0, blk == n_ctx_blocks)
        is_last = jnp.logical_or(blk == n_ctx_blocks - 1, blk == n_blocks - 1)
        prev = jnp.where(is_first, 0.0, xp_ref[...].astype(F32))
        nxt = jnp.where(is_last, 0.0, xn_ref[...].astype(F32))
        xe = jnp.concatenate([prev, x_ref[...].astype(F32), nxt], axis=0)
        u = cb
        for tap in range(CONV_W):
            off = halo - (2 - tap) * BATCH
            u = u + xe[off:off + rows] * cw[tap:tap + 1]
        t = jnp.tanh(jnp.dot(u.astype(BF16), gw_ref[d], preferred_element_type=F32) + gb_ref[d])
        lam = lam_ref[d]
        softplus = jnp.maximum(-lam, 0.0) + jnp.log1p(jnp.exp(-jnp.abs(lam)))
        half_c = (-0.5 * LRU_C) * softplus
        log_a = half_c * t[:, :LRU_CW] + half_c
        half_u = 0.5 * u
        iu = half_u * t[:, LRU_CW:] + half_u
        th = jnp.tanh(log_a)
        w = (-2.0 * th) / (1.0 - th)
        a_scr[d] = jnp.exp(log_a)
        b_scr[d] = jnp.where(w > 0.0, w * lax.rsqrt(w), 0.0) * iu

    prepare(0, j, xf_ref, xfp_ref, xfn_ref)
    prepare(1, jb, xb_ref, xbp_ref, xbn_ref)

    @pl.when(j == 0)
    def _():
        h_scr[...] = jnp.zeros_like(h_scr)

    n_steps = rows // BATCH

    def step(s, carry):
        hf, hb = carry
        rf = pl.ds(pl.multiple_of(s * BATCH, BATCH), BATCH)
        hf = a_scr[0, rf, :] * hf + b_scr[0, rf, :]
        hf_ref[rf, :] = hf
        rb = pl.ds(pl.multiple_of((n_steps - 1 - s) * BATCH, BATCH), BATCH)
        hb = a_scr[1, rb, :] * hb + b_scr[1, rb, :]
        hb_ref[rb, :] = hb
        return hf, hb

    hf, hb = lax.fori_loop(0, n_steps, step, (h_scr[0], h_scr[1]), unroll=8)
    h_scr[0] = hf
    h_scr[1] = hb


def _rg_lru(z, conv_w, conv_b, gate_w, gate_b, lam, n_rows, n_ctx_blocks):
    rows = LRU_TB * BATCH
    n_blocks = n_rows // rows
    n_cc = D_MODEL // LRU_CW
    per_chunk = PROJ_CHUNK // LRU_CW
    halo = 16
    hb_per_block = rows // halo
    n_halo = n_rows // halo

    def bwd_block(j):
        return jnp.where(j < n_ctx_blocks, n_ctx_blocks - 1 - j, n_blocks - 1 + n_ctx_blocks - j)

    def cur(f):
        return pl.BlockSpec((None, rows, LRU_CW), lambda c, j: (2 + c // per_chunk, f(j), c % per_chunk))

    def prev(f):
        return pl.BlockSpec((None, halo, LRU_CW),
                            lambda c, j: (2 + c // per_chunk, jnp.maximum(f(j) * hb_per_block - 1, 0),
                                          c % per_chunk))

    def nxt(f):
        return pl.BlockSpec((None, halo, LRU_CW),
                            lambda c, j: (2 + c // per_chunk,
                                          jnp.minimum((f(j) + 1) * hb_per_block, n_halo - 1),
                                          c % per_chunk))

    ident = lambda j: j
    return pl.pallas_call(
        functools.partial(_lru_kernel, n_ctx_blocks=n_ctx_blocks, n_blocks=n_blocks),
        grid=(n_cc, n_blocks),
        in_specs=[
            cur(ident), prev(ident), nxt(ident), cur(bwd_block), prev(bwd_block), nxt(bwd_block),
            pl.BlockSpec((CONV_W, LRU_CW), lambda c, j: (0, c)),
            pl.BlockSpec((1, LRU_CW), lambda c, j: (0, c)),
            pl.BlockSpec((2, None, LRU_CW, 2 * LRU_CW), lambda c, j: (0, c, 0, 0)),
            pl.BlockSpec((2, None, 1, 2 * LRU_CW), lambda c, j: (0, c, 0, 0)),
            pl.BlockSpec((2, None, 1, LRU_CW), lambda c, j: (0, c, 0, 0)),
        ],
        out_specs=[
            pl.BlockSpec((rows, LRU_CW), lambda c, j: (j, c)),
            pl.BlockSpec((rows, LRU_CW), lambda c, j: (bwd_block(j), c)),
        ],
        out_shape=[jax.ShapeDtypeStruct((n_rows, D_MODEL), F32)] * 2,
        scratch_shapes=[
            pltpu.VMEM((2, rows, LRU_CW), F32),
            pltpu.VMEM((2, rows, LRU_CW), F32),
            pltpu.VMEM((2, BATCH, LRU_CW), F32),
        ],
        compiler_params=_cparams("parallel", "arbitrary"),
        name="rg_lru",
    )(z, z, z, z, z, z, conv_w, conv_b, gate_w, gate_b, lam)


def _gelu_tanh(x):
    return 0.5 * x * (1.0 + jnp.tanh(0.7978845608028654 * (x + 0.044715 * (x * x * x))))


def _sequences_to_rows(o_ref, slab_scr):
    steps = o_ref.shape[0]
    n_slabs = slab_scr.shape[0]
    width = n_slabs * LANES
    for b in range(BATCH):
        for s in range(n_slabs):
            col = b * width + s * LANES
            slab_scr[s, pl.ds(b, steps, stride=BATCH), :] = o_ref[:, col:col + LANES].astype(F32)


def _merge_kernel(x_ref, mod_ref, o_ref_in, hf_ref, hb_ref, gr0_ref, gr1_ref, gl0_ref, gl1_ref,
                  ga0_ref, ga1_ref, gb0_ref, gb1_ref, wr_ref, wl_ref, wo_ref, o_ref, slab_scr):
    rows = x_ref.shape[0]
    n_slabs = slab_scr.shape[0]
    _sequences_to_rows(o_ref_in, slab_scr)

    def both(r0, r1):
        return jnp.concatenate([r0[...], r1[...]], axis=1).astype(F32)

    gr = both(gr0_ref, gr1_ref)
    o_ret = jnp.concatenate([slab_scr[s] for s in range(n_slabs)], axis=1) * (gr * _sigmoid(gr))
    y_a = jnp.dot(o_ret.astype(BF16), wr_ref[...], preferred_element_type=F32)
    h = hf_ref[...] + hb_ref[...]
    y_b = jnp.dot((h * _gelu_tanh(both(gl0_ref, gl1_ref))).astype(BF16), wl_ref[...],
                  preferred_element_type=F32)
    m = (_sigmoid(both(ga0_ref, ga1_ref)) * y_a + _sigmoid(both(gb0_ref, gb1_ref)) * y_b).astype(BF16)
    y = jnp.dot(m, wo_ref[...], preferred_element_type=F32)
    y3 = y.reshape(rows // BATCH, BATCH, D_MODEL)
    x3 = x_ref[...].reshape(rows // BATCH, BATCH, D_MODEL)
    o_ref[...] = (x3 + mod_ref[2][None] * y3).reshape(rows, D_MODEL)


def _merge(xs, tab, o_seq, hf, hb, z, w_ret_o, w_lru_o, w_out, ctx_rows, row_start):
    n = xs.shape[0]
    tm = ROW_TILE
    t0 = row_start // tm

    def zc(c):
        return pl.BlockSpec((None, tm, PROJ_CHUNK), lambda i: (c, i + t0, 0))

    row = lambda w: pl.BlockSpec((tm, w), lambda i: (i + t0, 0))
    return pl.pallas_call(
        _merge_kernel,
        grid=(n // tm - t0,),
        in_specs=[
            row(D_MODEL),
            pl.BlockSpec((None, 3, BATCH, D_MODEL),
                         lambda i: (((i + t0) * tm >= ctx_rows).astype(jnp.int32), 1, 0, 0)),
            pl.BlockSpec((tm // BATCH, BATCH * D_MODEL), lambda i: (i + t0, 0)),
            row(D_MODEL), row(D_MODEL),
            zc(0), zc(1), zc(4), zc(5), zc(6), zc(7), zc(8), zc(9),
            _resident((D_MODEL, D_MODEL)), _resident((D_MODEL, D_MODEL)), _resident((D_MODEL, D_MODEL)),
        ],
        out_specs=pl.BlockSpec((tm, D_MODEL), lambda i: (i, 0)),
        out_shape=jax.ShapeDtypeStruct((n - row_start, D_MODEL), F32),
        scratch_shapes=[pltpu.VMEM((D_MODEL // LANES, tm, LANES), F32)],
        compiler_params=_cparams("parallel"),
        name="merge_out_proj",
    )(xs, tab, o_seq, hf, hb, z, z, z, z, z, z, z, z, w_ret_o, w_lru_o, w_out)


def _final_kernel(x_ref, g_ref, o_ref, slab_scr):
    x = x_ref[...]
    steps = x.shape[0] // BATCH
    ms = jnp.mean(x * x, axis=-1, keepdims=True)
    y = (x * lax.rsqrt(ms + EPS)) * g_ref[...]
    n_slabs = slab_scr.shape[0]
    for s in range(n_slabs):
        slab_scr[s] = y[:, s * LANES:(s + 1) * LANES]
    for b in range(BATCH):
        for s in range(n_slabs):
            o_ref[b, :, s * LANES:(s + 1) * LANES] = slab_scr[s, pl.ds(b, steps, stride=BATCH), :]


def _final_norm(xs, g):
    n = xs.shape[0]
    tm = ROW_TILE
    return pl.pallas_call(
        _final_kernel,
        grid=(n // tm,),
        in_specs=[
            pl.BlockSpec((tm, D_MODEL), lambda i: (i, 0)),
            pl.BlockSpec((1, D_MODEL), lambda i: (0, 0)),
        ],
        out_specs=pl.BlockSpec((BATCH, tm // BATCH, D_MODEL), lambda i: (0, i, 0)),
        out_shape=jax.ShapeDtypeStruct((BATCH, n // BATCH, D_MODEL), F32),
        scratch_shapes=[pltpu.VMEM((D_MODEL // LANES, tm, LANES), F32)],
        compiler_params=_cparams("parallel"),
        name="final_norm",
    )(xs, g)


def _time_major_kernel(ctx_ref, x_ref, o_ref, slab_scr, *, n_ctx_tiles):
    steps = o_ref.shape[0] // BATCH
    n_slabs = slab_scr.shape[0]

    def emit(src_ref):
        for b in range(BATCH):
            for s in range(n_slabs):
                slab_scr[s, pl.ds(b, steps, stride=BATCH), :] = src_ref[b, :, s * LANES:(s + 1) * LANES]
        o_ref[...] = jnp.concatenate([slab_scr[s] for s in range(n_slabs)], axis=1)

    @pl.when(pl.program_id(0) < n_ctx_tiles)
    def _():
        emit(ctx_ref)

    @pl.when(pl.program_id(0) >= n_ctx_tiles)
    def _():
        emit(x_ref)


def _to_time_major(ctx, x):
    t_ctx, t_lat = ctx.shape[1], x.shape[1]
    tm = ROW_TILE
    steps = tm // BATCH
    n_ctx_tiles = t_ctx // steps
    n = (t_ctx + t_lat) * BATCH
    return pl.pallas_call(
        functools.partial(_time_major_kernel, n_ctx_tiles=n_ctx_tiles),
        grid=(n // tm,),
        in_specs=[
            pl.BlockSpec((BATCH, steps, D_MODEL), lambda i: (0, jnp.minimum(i, n_ctx_tiles - 1), 0)),
            pl.BlockSpec((BATCH, steps, D_MODEL), lambda i: (0, jnp.maximum(i - n_ctx_tiles, 0), 0)),
        ],
        out_specs=pl.BlockSpec((tm, D_MODEL), lambda i: (i, 0)),
        out_shape=jax.ShapeDtypeStruct((n, D_MODEL), F32),
        scratch_shapes=[pltpu.VMEM((D_MODEL // LANES, tm, LANES), F32)],
        compiler_params=_cparams("parallel"),
        name="to_time_major",
    )(ctx, x)


def _rotary_tables(t_ctx, t_lat):
    pos = jnp.arange(t_lat, dtype=jnp.int32)
    row = (pos // GRID_W).astype(F32)
    col = (pos % GRID_W).astype(F32)
    n_f = RET_DK // 4
    inv = ROPE_BASE ** (-jnp.arange(n_f, dtype=F32) / n_f)
    ang = jnp.concatenate([row[:, None] * inv, col[:, None] * inv], axis=-1)
    cos = jnp.concatenate([jnp.ones((t_ctx, RET_DK // 2), F32), jnp.cos(ang)], axis=0)
    sin = jnp.concatenate([jnp.zeros((t_ctx, RET_DK // 2), F32), jnp.sin(ang)], axis=0)
    cos128 = jnp.tile(cos, (1, 4))
    sin128 = jnp.tile(jnp.concatenate([-sin, sin], axis=1), (1, 2))
    return jnp.repeat(cos128, BATCH, axis=0), jnp.repeat(sin128, BATCH, axis=0)


def _lru_gate_weights(gate_w, gate_b):
    n_cc = D_MODEL // LRU_CW
    per = LRU_CW // LRU_BW
    gate_w = 0.5 * gate_w
    gate_b = 0.5 * gate_b
    w = gate_w.reshape(2, 2, n_cc, per, LRU_BW, LRU_BW)
    eye = jnp.eye(per, dtype=gate_w.dtype)
    bd = jnp.einsum('dgcpij,pq->dgcpiqj', w, eye).reshape(2, 2, n_cc, LRU_CW, LRU_CW)
    wcat = jnp.concatenate([bd[:, 0], bd[:, 1]], axis=-1).astype(BF16)
    b = gate_b.reshape(2, 2, n_cc, 1, LRU_CW)
    bcat = jnp.concatenate([b[:, 0], b[:, 1]], axis=-1)
    return wcat, bcat


def kernel(x, c, ctx, c_ctx, w_mod, b_mod, norm_g, ffn1_w_gu, ffn1_w_down, ffn2_w_gu, ffn2_w_down,
           w_in, ret_decay_logit, w_ret_o, lru_conv_w, lru_conv_b, lru_gate_w, lru_gate_b,
           lru_lambda, w_lru_o, w_out, final_g):
    depth = w_mod.shape[0]
    t_lat, t_ctx = x.shape[1], ctx.shape[1]
    t_all = t_lat + t_ctx
    n_rows = t_all * BATCH
    ctx_rows = t_ctx * BATCH
    assert x.shape[0] == BATCH and x.shape[2] == D_MODEL
    assert t_ctx % LRU_TB == 0 and t_lat % LRU_TB == 0 and ctx_rows % ROW_TILE == 0

    xs = _to_time_major(ctx, x)
    tabs = _mod_tables(c, c_ctx, w_mod, b_mod)
    cos_t, sin_t = _rotary_tables(t_ctx, t_lat)
    n_cc = D_MODEL // LRU_CW

    for l in range(depth):
        last = l == depth - 1
        tab = tabs[l]
        g = norm_g[l].reshape(3, 1, D_MODEL)
        xs = _ffn_sublayer(xs, tab, 0, g[0], ffn1_w_gu, ffn1_w_down, l, ctx_rows)
        q, k, v, z = _in_projection(xs, tab, g[1], cos_t, sin_t, w_in, l, ctx_rows)
        logit_rows = jnp.broadcast_to(ret_decay_logit[l].astype(F32)[:, :, None, None],
                                      (2, RET_HEADS, 1, LANES))
        o_seq = _retention(q, k, v, logit_rows, t_ctx // RET_CHUNK)
        gw, gb = _lru_gate_weights(lru_gate_w[l], lru_gate_b[l])
        hf, hb = _rg_lru(z, lru_conv_w[l], lru_conv_b[l].reshape(1, D_MODEL), gw, gb,
                         lru_lambda[l].reshape(2, n_cc, 1, LRU_CW), n_rows, t_ctx // LRU_TB)
        row_start = ctx_rows if last else 0
        xs = _merge(xs, tab, o_seq, hf, hb, z, w_ret_o[l].astype(BF16), w_lru_o[l].astype(BF16),
                    w_out[l].astype(BF16), ctx_rows, row_start)
        xs = _ffn_sublayer(xs, tab, 6, g[2], ffn2_w_gu, ffn2_w_down, l, ctx_rows - row_start)
    return _final_norm(xs, final_g.reshape(1, D_MODEL))
```

```python
import functools

import jax
import jax.numpy as jnp
from jax import lax
from jax.experimental import pallas as pl
from jax.experimental.pallas import tpu as pltpu

F32 = jnp.float32
BF16 = jnp.bfloat16

D_MODEL = 1024
BATCH = 8
LANES = 128
RET_HEADS = 8
RET_DK = 64
RET_DV = 128
RET_CHUNK = 128
HEAD_GROUP = 4
GRID_W = 64
ROPE_BASE = 10000.0
LRU_BLOCKS = 16
LRU_BW = D_MODEL // LRU_BLOCKS
LRU_C = 8.0
LRU_GW = 256
LRU_CW = 512
LRU_TB = 128
CONV_W = 4
FFN_HIDDEN = 2816
FFN_TF = 256
FFN_RES = 0.5
N_MOD = 9
EPS = 1e-6
PROJ_CHUNK = 512
N_PROJ_CHUNKS = 14
ROW_TILE = 512
VMEM_LIMIT = 52 * 1024 * 1024


def _cparams(*sem):
    return pltpu.CompilerParams(dimension_semantics=sem, vmem_limit_bytes=VMEM_LIMIT)


def _resident(shape):
    return pl.BlockSpec(shape, lambda *_: (0,) * len(shape), pipeline_mode=pl.Buffered(1))


def _sigmoid(x):
    return 1.0 / (1.0 + jnp.exp(-x))


def _ada_norm(x, g, shift, scale):
    rows = x.shape[0]
    ms = jnp.mean(x * x, axis=-1, keepdims=True)
    y = (x * lax.rsqrt(ms + EPS)) * g
    y3 = y.reshape(rows // BATCH, BATCH, D_MODEL)
    h = y3 * (1.0 + scale)[None] + shift[None]
    return h.reshape(rows, D_MODEL)


def _mod_kernel(c_ref, w_ref, b_ref, o_ref):
    c = c_ref[...]
    s = (c * _sigmoid(c)).astype(BF16)
    o_ref[...] = jnp.dot(s, w_ref[...].astype(BF16), preferred_element_type=F32) + b_ref[...]


def _mod_tables(c, c_ctx, w_mod, b_mod):
    depth = w_mod.shape[0]
    cc = jnp.zeros((2 * BATCH, D_MODEL), F32).at[:BATCH].set(c).at[BATCH].set(c_ctx)
    out = pl.pallas_call(
        _mod_kernel,
        grid=(depth, N_MOD),
        in_specs=[
            pl.BlockSpec((2 * BATCH, D_MODEL), lambda l, j: (0, 0)),
            pl.BlockSpec((None, D_MODEL, D_MODEL), lambda l, j: (l, 0, j)),
            pl.BlockSpec((None, 1, D_MODEL), lambda l, j: (l, 0, j)),
        ],
        out_specs=pl.BlockSpec((None, 2 * BATCH, D_MODEL), lambda l, j: (l, 0, j)),
        out_shape=jax.ShapeDtypeStruct((depth, 2 * BATCH, N_MOD * D_MODEL), F32),
        compiler_params=_cparams("parallel", "parallel"),
        name="adaln_mod",
    )(cc, w_mod, b_mod.reshape(depth, 1, N_MOD * D_MODEL))
    out = out.reshape(depth, 2 * BATCH, N_MOD, D_MODEL)
    lat = out[:, :BATCH].transpose(0, 2, 1, 3)
    ctx = jnp.broadcast_to(out[:, BATCH][:, :, None, :], lat.shape)
    return jnp.stack([ctx, lat], axis=1)


def _ffn_kernel(x_ref, mod_ref, g_ref, wg32_ref, wu32_ref, wd32_ref, o_ref,
                wg_s, wu_s, wd_s, h_scr, act_scr, acc_scr):
    nf = FFN_HIDDEN // FFN_TF
    step = pl.program_id(0)

    @pl.when(step < nf)
    def _():
        wg_s[step] = wg32_ref[...].astype(BF16)
        wu_s[step] = wu32_ref[...].astype(BF16)
        wd_s[step] = wd32_ref[...].astype(BF16)

    @pl.when(step >= nf)
    def _():
        rows = x_ref.shape[0]
        x = x_ref[...]
        h_scr[...] = _ada_norm(x, g_ref[...], mod_ref[0], mod_ref[1]).astype(BF16)
        acc_scr[...] = jnp.zeros_like(acc_scr)

        def gated(k):
            hb = h_scr[...]
            u = jnp.dot(hb, wg_s[k], preferred_element_type=F32)
            v = jnp.dot(hb, wu_s[k], preferred_element_type=F32)
            return ((u * _sigmoid(u)) * v).astype(BF16)

        def down(k):
            return jnp.dot(act_scr[...], wd_s[k], preferred_element_type=F32)

        act_scr[...] = gated(0)
        for k in range(1, nf):
            part = down(k - 1)
            act_scr[...] = gated(k)
            acc_scr[...] += part
        acc_scr[...] += down(nf - 1)
        y3 = acc_scr[...].reshape(rows // BATCH, BATCH, D_MODEL)
        x3 = x.reshape(rows // BATCH, BATCH, D_MODEL)
        o_ref[...] = (x3 + (FFN_RES * mod_ref[2])[None] * y3).reshape(rows, D_MODEL)


def _ffn_sublayer(xs, tab, sub, g, w_gu, w_down, layer, ctx_rows):
    n = xs.shape[0]
    tm = ROW_TILE
    nf = FFN_HIDDEN // FFN_TF
    tile = lambda i: jnp.maximum(i - nf, 0)
    chunk = lambda i: jnp.minimum(i, nf - 1)
    return pl.pallas_call(
        _ffn_kernel,
        grid=(nf + n // tm,),
        in_specs=[
            pl.BlockSpec((tm, D_MODEL), lambda i: (tile(i), 0)),
            pl.BlockSpec((None, 3, BATCH, D_MODEL),
                         lambda i: ((tile(i) * tm >= ctx_rows).astype(jnp.int32), sub // 3, 0, 0)),
            _resident((1, D_MODEL)),
            pl.BlockSpec((None, D_MODEL, FFN_TF), lambda i: (layer, 0, chunk(i))),
            pl.BlockSpec((None, D_MODEL, FFN_TF), lambda i: (layer, 0, nf + chunk(i))),
            pl.BlockSpec((None, FFN_TF, D_MODEL), lambda i: (layer, chunk(i), 0)),
        ],
        out_specs=pl.BlockSpec((tm, D_MODEL), lambda i: (tile(i), 0)),
        out_shape=jax.ShapeDtypeStruct((n, D_MODEL), F32),
        scratch_shapes=[pltpu.VMEM((nf, D_MODEL, FFN_TF), BF16), pltpu.VMEM((nf, D_MODEL, FFN_TF), BF16),
                        pltpu.VMEM((nf, FFN_TF, D_MODEL), BF16),
                        pltpu.VMEM((tm, D_MODEL), BF16), pltpu.VMEM((tm, FFN_TF), BF16),
                        pltpu.VMEM((tm, D_MODEL), F32)],
        compiler_params=_cparams("arbitrary"),
        name="ffn_sublayer",
    )(xs, tab, g, w_gu, w_gu, w_down)


def _swap_halves(a):
    w = a.shape[1]
    lane = lax.broadcasted_iota(jnp.int32, a.shape, 1)
    first_half = (lane % RET_DK) < (RET_DK // 2)
    return jnp.where(first_half, pltpu.roll(a, w - RET_DK // 2, 1), pltpu.roll(a, RET_DK // 2, 1))


def _rows_to_sequences(a, slab_scr, out_ref):
    rows, width = a.shape
    steps = rows // BATCH
    n_slabs = width // LANES
    for s in range(n_slabs):
        slab_scr[s] = a[:, s * LANES:(s + 1) * LANES]
    for b in range(BATCH):
        for s in range(n_slabs):
            col = b * width + s * LANES
            out_ref[:, col:col + LANES] = slab_scr[s, pl.ds(b, steps, stride=BATCH), :].astype(out_ref.dtype)


def _inproj_kernel(x_ref, mod_ref, g_ref, cos_ref, sin_ref, w32_ref, q_ref, k_ref, v_ref, z_ref,
                   w_s, h_scr, slab_scr):
    step = pl.program_id(0)

    @pl.when(step < N_PROJ_CHUNKS)
    def _():
        w_s[step] = w32_ref[...].astype(BF16)

    @pl.when(step >= N_PROJ_CHUNKS)
    def _():
        h_scr[...] = _ada_norm(x_ref[...], g_ref[...], mod_ref[0], mod_ref[1]).astype(BF16)
        reps = PROJ_CHUNK // cos_ref.shape[1]
        cos = jnp.tile(cos_ref[...], (1, reps))
        sin = jnp.tile(sin_ref[...], (1, reps))

        def rotate(a):
            return a * cos + _swap_halves(a) * sin

        def proj(c):
            return jnp.dot(h_scr[...], w_s[c], preferred_element_type=F32)

        _rows_to_sequences(rotate(proj(0)), slab_scr, q_ref)
        _rows_to_sequences(rotate(proj(1)) * (RET_DK ** -0.5), slab_scr, k_ref)
        for c in range(2):
            _rows_to_sequences(proj(2 + c), slab_scr, v_ref.at[c])
        for c in range(4, N_PROJ_CHUNKS):
            z_ref[c - 4] = proj(c).astype(BF16)


def _in_projection(xs, tab, g, cos_t, sin_t, w_in, layer, ctx_rows):
    n = xs.shape[0]
    tm = ROW_TILE
    nz = N_PROJ_CHUNKS - 4
    steps = tm // BATCH
    t_all = n // BATCH
    seq_w = BATCH * PROJ_CHUNK
    nw = N_PROJ_CHUNKS
    tile = lambda i: jnp.maximum(i - nw, 0)
    return pl.pallas_call(
        _inproj_kernel,
        grid=(nw + n // tm,),
        in_specs=[
            pl.BlockSpec((tm, D_MODEL), lambda i: (tile(i), 0)),
            pl.BlockSpec((None, 3, BATCH, D_MODEL),
                         lambda i: ((tile(i) * tm >= ctx_rows).astype(jnp.int32), 1, 0, 0)),
            _resident((1, D_MODEL)),
            pl.BlockSpec((tm, LANES), lambda i: (tile(i), 0)),
            pl.BlockSpec((tm, LANES), lambda i: (tile(i), 0)),
            pl.BlockSpec((None, D_MODEL, PROJ_CHUNK), lambda i: (layer, 0, jnp.minimum(i, nw - 1))),
        ],
        out_specs=[
            pl.BlockSpec((steps, seq_w), lambda i: (tile(i), 0)),
            pl.BlockSpec((steps, seq_w), lambda i: (tile(i), 0)),
            pl.BlockSpec((2, steps, seq_w), lambda i: (0, tile(i), 0)),
            pl.BlockSpec((nz, tm, PROJ_CHUNK), lambda i: (0, tile(i), 0)),
        ],
        out_shape=[
            jax.ShapeDtypeStruct((t_all, seq_w), BF16),
            jax.ShapeDtypeStruct((t_all, seq_w), BF16),
            jax.ShapeDtypeStruct((2, t_all, seq_w), BF16),
            jax.ShapeDtypeStruct((nz, n, PROJ_CHUNK), BF16),
        ],
        scratch_shapes=[pltpu.VMEM((N_PROJ_CHUNKS, D_MODEL, PROJ_CHUNK), BF16),
                        pltpu.VMEM((tm, D_MODEL), BF16),
                        pltpu.VMEM((PROJ_CHUNK // LANES, tm, LANES), F32)],
        compiler_params=_cparams("arbitrary"),
        name="in_projection",
    )(xs, tab, g, cos_t, sin_t, w_in)


def _log_sigmoid(x):
    return jnp.minimum(x, 0.0) - jnp.log1p(jnp.exp(-jnp.abs(x)))


def _dot_t1(a, b):
    return lax.dot_general(a, b, (((1,), (1,)), ((), ())), preferred_element_type=F32)


def _retention_kernel(lg_ref, q_ref, k_ref, v_ref, o_ref,
                      dmat, qdec, kdec, cdec, sb_scr, sf_scr, *, n_ctx_chunks):
    c_len = RET_CHUNK
    pair_k = 2 * RET_DK
    pair_v = 2 * RET_DV
    n_pairs = HEAD_GROUP // 2
    n_chunks = q_ref.shape[0] // c_len
    hg = pl.program_id(1)
    row_i = lax.broadcasted_iota(jnp.int32, (c_len, c_len), 0)
    col_i = lax.broadcasted_iota(jnp.int32, (c_len, c_len), 1)
    row = row_i.astype(F32)
    col = col_i.astype(F32)
    rel = row - col
    first_lanes = col_i < RET_DK
    first_rows = row_i < RET_DK
    own_block = (lax.broadcasted_iota(jnp.int32, (pair_k, pair_v), 0) < RET_DK) == (
        lax.broadcasted_iota(jnp.int32, (pair_k, pair_v), 1) < RET_DV)

    for p in range(n_pairs):
        h0 = hg * HEAD_GROUP + 2 * p
        lg = [[_log_sigmoid(lg_ref[d, h0 + e]) for e in range(2)] for d in range(2)]
        for e in range(2):
            dmat[2 * p + e] = jnp.where(rel >= 0.0, jnp.exp(lg[0][e] * jnp.maximum(rel, 0.0)),
                                        jnp.exp(lg[1][e] * jnp.maximum(-rel, 0.0)))
        lane_f = jnp.where(first_lanes, lg[0][0], lg[0][1])
        lane_b = jnp.where(first_lanes, lg[1][0], lg[1][1])
        row_f = jnp.where(first_rows, lg[0][0], lg[0][1])
        row_b = jnp.where(first_rows, lg[1][0], lg[1][1])
        qdec[0, p] = jnp.exp(lane_f * (row + 1.0))
        qdec[1, p] = jnp.exp(lane_b * (c_len - row))
        kdec[0, p] = jnp.exp(row_f * (c_len - 1.0 - col))
        kdec[1, p] = jnp.exp(row_b * col)
        cdec[0, p] = jnp.exp(row_f * c_len)
        cdec[1, p] = jnp.exp(row_b * c_len)

    def advance(d, p, s, kp, vp):
        kd = (kp.astype(F32).T * kdec[d, p]).astype(BF16)
        upd = jnp.dot(kd, vp, preferred_element_type=F32)
        cd = cdec[d, p]
        return s * jnp.concatenate([cd, cd], axis=1) + jnp.where(own_block, upd, 0.0)

    sf_scr[...] = jnp.zeros_like(sf_scr)

    def bwd_body(i, carry):
        c = jnp.where(i < n_ctx_chunks, n_ctx_chunks - 1 - i, n_chunks - 1 + n_ctx_chunks - i)
        rows = pl.ds(pl.multiple_of(c * c_len, c_len), c_len)
        for p in range(n_pairs):
            kp = k_ref[rows, p * pair_k:(p + 1) * pair_k]
            vp = v_ref[rows, p * pair_v:(p + 1) * pair_v]
            s = sf_scr[p]
            sb_scr[c, p] = s.astype(BF16)
            sf_scr[p] = advance(1, p, s, kp, vp)
        return carry

    lax.fori_loop(0, n_chunks, bwd_body, 0, unroll=3)

    sf_scr[...] = jnp.zeros_like(sf_scr)
    zero_v = jnp.zeros((c_len, RET_DV), BF16)
    zero_q = jnp.zeros((c_len, pair_k), BF16)

    def fwd_body(c, carry):
        rows = pl.ds(pl.multiple_of(c * c_len, c_len), c_len)
        for p in range(n_pairs):
            qp = q_ref[rows, p * pair_k:(p + 1) * pair_k]
            kp = k_ref[rows, p * pair_k:(p + 1) * pair_k]
            vp = v_ref[rows, p * pair_v:(p + 1) * pair_v]
            q_split = jnp.concatenate([jnp.where(first_lanes, qp, zero_q),
                                       jnp.where(first_lanes, zero_q, qp)], axis=0)
            scores = _dot_t1(q_split, kp)
            pa = (scores[:c_len] * dmat[2 * p]).astype(BF16)
            pb = (scores[c_len:] * dmat[2 * p + 1]).astype(BF16)
            qf = qp.astype(F32)
            s = sf_scr[p]
            lhs = jnp.concatenate([pa, pb, (qf * qdec[0, p]).astype(BF16), (qf * qdec[1, p]).astype(BF16)],
                                  axis=1)
            rhs = jnp.concatenate([
                jnp.concatenate([vp[:, :RET_DV], zero_v], axis=1),
                jnp.concatenate([zero_v, vp[:, RET_DV:]], axis=1),
                s.astype(BF16), sb_scr[c, p]], axis=0)
            o = jnp.dot(lhs, rhs, preferred_element_type=F32)
            sf_scr[p] = advance(0, p, s, kp, vp)
            for e in range(2):
                oe = o[:, e * RET_DV:(e + 1) * RET_DV]
                mu = jnp.mean(oe, axis=-1, keepdims=True)
                d = oe - mu
                var = jnp.mean(d * d, axis=-1, keepdims=True)
                col0 = p * pair_v + e * RET_DV
                o_ref[rows, col0:col0 + RET_DV] = (d * lax.rsqrt(var + EPS)).astype(BF16)
        return carry

    lax.fori_loop(0, n_chunks, fwd_body, 0, unroll=3)


def _retention(q, k, v, logit_rows, n_ctx_chunks):
    t_all = q.shape[0]
    n_groups = RET_HEADS // HEAD_GROUP
    n_pairs = HEAD_GROUP // 2
    qw = HEAD_GROUP * RET_DK
    vw = HEAD_GROUP * RET_DV
    assert vw == PROJ_CHUNK and n_groups == v.shape[0]
    n_chunks = t_all // RET_CHUNK
    return pl.pallas_call(
        functools.partial(_retention_kernel, n_ctx_chunks=n_ctx_chunks),
        grid=(BATCH, n_groups),
        in_specs=[
            _resident((2, RET_HEADS, 1, LANES)),
            pl.BlockSpec((t_all, qw), lambda b, h: (0, b * n_groups + h)),
            pl.BlockSpec((t_all, qw), lambda b, h: (0, b * n_groups + h)),
            pl.BlockSpec((None, t_all, vw), lambda b, h: (h, 0, b)),
        ],
        out_specs=pl.BlockSpec((t_all, vw), lambda b, h: (0, b * n_groups + h)),
        out_shape=jax.ShapeDtypeStruct((t_all, BATCH * RET_HEADS * RET_DV), BF16),
        scratch_shapes=[
            pltpu.VMEM((HEAD_GROUP, RET_CHUNK, RET_CHUNK), F32),
            pltpu.VMEM((2, n_pairs, RET_CHUNK, 2 * RET_DK), F32),
            pltpu.VMEM((2, n_pairs, 2 * RET_DK, RET_CHUNK), F32),
            pltpu.VMEM((2, n_pairs, 2 * RET_DK, RET_CHUNK), F32),
            pltpu.VMEM((n_chunks, n_pairs, 2 * RET_DK, 2 * RET_DV), BF16),
            pltpu.VMEM((n_pairs, 2 * RET_DK, 2 * RET_DV), F32),
        ],
        compiler_params=_cparams("parallel", "parallel"),
        name="retention",
    )(logit_rows, q, k, v)


def _lru_kernel(xf_ref, xfp_ref, xfn_ref, xb_ref, xbp_ref, xbn_ref,
                cw_ref, cb_ref, gw_ref, gb_ref, lam_ref, hf_ref, hb_ref,
                a_scr, b_scr, h_scr, *, n_ctx_blocks, n_blocks):
    j = pl.program_id(1)
    jb = jnp.where(j < n_ctx_blocks, n_ctx_blocks - 1 - j, n_blocks - 1 + n_ctx_blocks - j)
    rows = xf_ref.shape[0]
    halo = xfp_ref.shape[0]
    cw = cw_ref[...]
    cb = cb_ref[...]

    def prepare(d, blk, x_ref, xp_ref, xn_ref):
        is_first = jnp.logical_or(blk == 0, blk == n_ctx_blocks)
        is_last = jnp.logical_or(blk == n_ctx_blocks - 1, blk == n_blocks - 1)
        prev = jnp.where(is_first, 0.0, xp_ref[...].astype(F32))
        nxt = jnp.where(is_last, 0.0, xn_ref[...].astype(F32))
        xe = jnp.concatenate([prev, x_ref[...].astype(F32), nxt], axis=0)
        u = cb
        for tap in range(CONV_W):
            off = halo - (2 - tap) * BATCH
            u = u + xe[off:off + rows] * cw[tap:tap + 1]
        for grp in range(LRU_CW // LRU_GW):
            cols = slice(grp * LRU_GW, (grp + 1) * LRU_GW)
            ug = u[:, cols]
            t = jnp.tanh(jnp.dot(ug.astype(BF16), gw_ref[d, grp], preferred_element_type=F32)
                         + gb_ref[d, grp])
            lam = lam_ref[d, grp]
            softplus = jnp.maximum(-lam, 0.0) + jnp.log1p(jnp.exp(-jnp.abs(lam)))
            half_c = (-0.5 * LRU_C) * softplus
            log_a = half_c * t[:, :LRU_GW] + half_c
            half_u = 0.5 * ug
            iu = half_u * t[:, LRU_GW:] + half_u
            th = jnp.tanh(log_a)
            w = (-2.0 * th) / (1.0 - th)
            a_scr[d, :, cols] = jnp.exp(log_a)
            b_scr[d, :, cols] = jnp.where(w > 0.0, w * lax.rsqrt(w), 0.0) * iu

    prepare(0, j, xf_ref, xfp_ref, xfn_ref)
    prepare(1, jb, xb_ref, xbp_ref, xbn_ref)

    @pl.when(j == 0)
    def _():
        h_scr[...] = jnp.zeros_like(h_scr)

    n_steps = rows // BATCH

    def step(s, carry):
        hf, hb = carry
        rf = pl.ds(pl.multiple_of(s * BATCH, BATCH), BATCH)
        hf = a_scr[0, rf, :] * hf + b_scr[0, rf, :]
        hf_ref[rf, :] = hf
        rb = pl.ds(pl.multiple_of((n_steps - 1 - s) * BATCH, BATCH), BATCH)
        hb = a_scr[1, rb, :] * hb + b_scr[1, rb, :]
        hb_ref[rb, :] = hb
        return hf, hb

    hf, hb = lax.fori_loop(0, n_steps, step, (h_scr[0], h_scr[1]), unroll=8)
    h_scr[0] = hf
    h_scr[1] = hb


def _rg_lru(z, conv_w, conv_b, gate_w, gate_b, lam, n_rows, n_ctx_blocks):
    rows = LRU_TB * BATCH
    n_blocks = n_rows // rows
    n_cc = D_MODEL // LRU_CW
    n_grp = LRU_CW // LRU_GW
    per_chunk = PROJ_CHUNK // LRU_CW
    halo = 16
    hb_per_block = rows // halo
    n_halo = n_rows // halo

    def bwd_block(j):
        return jnp.where(j < n_ctx_blocks, n_ctx_blocks - 1 - j, n_blocks - 1 + n_ctx_blocks - j)

    def cur(f):
        return pl.BlockSpec((None, rows, LRU_CW), lambda c, j: (2 + c // per_chunk, f(j), c % per_chunk))

    def prev(f):
        return pl.BlockSpec((None, halo, LRU_CW),
                            lambda c, j: (2 + c // per_chunk, jnp.maximum(f(j) * hb_per_block - 1, 0),
                                          c % per_chunk))

    def nxt(f):
        return pl.BlockSpec((None, halo, LRU_CW),
                            lambda c, j: (2 + c // per_chunk,
                                          jnp.minimum((f(j) + 1) * hb_per_block, n_halo - 1),
                                          c % per_chunk))

    ident = lambda j: j
    return pl.pallas_call(
        functools.partial(_lru_kernel, n_ctx_blocks=n_ctx_blocks, n_blocks=n_blocks),
        grid=(n_cc, n_blocks),
        in_specs=[
            cur(ident), prev(ident), nxt(ident), cur(bwd_block), prev(bwd_block), nxt(bwd_block),
            pl.BlockSpec((CONV_W, LRU_CW), lambda c, j: (0, c)),
            pl.BlockSpec((1, LRU_CW), lambda c, j: (0, c)),
            pl.BlockSpec((2, n_grp, LRU_GW, 2 * LRU_GW), lambda c, j: (0, c, 0, 0)),
            pl.BlockSpec((2, n_grp, 1, 2 * LRU_GW), lambda c, j: (0, c, 0, 0)),
            pl.BlockSpec((2, n_grp, 1, LRU_GW), lambda c, j: (0, c, 0, 0)),
        ],
        out_specs=[
            pl.BlockSpec((rows, LRU_CW), lambda c, j: (j, c)),
            pl.BlockSpec((rows, LRU_CW), lambda c, j: (bwd_block(j), c)),
        ],
        out_shape=[jax.ShapeDtypeStruct((n_rows, D_MODEL), F32)] * 2,
        scratch_shapes=[
            pltpu.VMEM((2, rows, LRU_CW), F32),
            pltpu.VMEM((2, rows, LRU_CW), F32),
            pltpu.VMEM((2, BATCH, LRU_CW), F32),
        ],
        compiler_params=_cparams("parallel", "arbitrary"),
        name="rg_lru",
    )(z, z, z, z, z, z, conv_w, conv_b, gate_w, gate_b, lam)


def _gelu_tanh(x):
    return 0.5 * x * (1.0 + jnp.tanh(0.7978845608028654 * (x + 0.044715 * (x * x * x))))


def _sequences_to_rows(o_ref, slab_scr):
    steps = o_ref.shape[0]
    n_slabs = slab_scr.shape[0]
    width = n_slabs * LANES
    for b in range(BATCH):
        for s in range(n_slabs):
            col = b * width + s * LANES
            slab_scr[s, pl.ds(b, steps, stride=BATCH), :] = o_ref[:, col:col + LANES].astype(F32)


def _merge_kernel(x_ref, mod_ref, o_ref_in, hf_ref, hb_ref, gr0_ref, gr1_ref, gl0_ref, gl1_ref,
                  ga0_ref, ga1_ref, gb0_ref, gb1_ref, wr_ref, wl_ref, wo_ref, o_ref, slab_scr):
    rows = x_ref.shape[0]
    n_slabs = slab_scr.shape[0]
    _sequences_to_rows(o_ref_in, slab_scr)

    def both(r0, r1):
        return jnp.concatenate([r0[...], r1[...]], axis=1).astype(F32)

    gr = both(gr0_ref, gr1_ref)
    o_ret = jnp.concatenate([slab_scr[s] for s in range(n_slabs)], axis=1) * (gr * _sigmoid(gr))
    y_a = jnp.dot(o_ret.astype(BF16), wr_ref[...], preferred_element_type=F32)
    h = hf_ref[...] + hb_ref[...]
    y_b = jnp.dot((h * _gelu_tanh(both(gl0_ref, gl1_ref))).astype(BF16), wl_ref[...],
                  preferred_element_type=F32)
    m = (_sigmoid(both(ga0_ref, ga1_ref)) * y_a + _sigmoid(both(gb0_ref, gb1_ref)) * y_b).astype(BF16)
    y = jnp.dot(m, wo_ref[...], preferred_element_type=F32)
    y3 = y.reshape(rows // BATCH, BATCH, D_MODEL)
    x3 = x_ref[...].reshape(rows // BATCH, BATCH, D_MODEL)
    o_ref[...] = (x3 + mod_ref[2][None] * y3).reshape(rows, D_MODEL)


def _merge(xs, tab, o_seq, hf, hb, z, w_ret_o, w_lru_o, w_out, ctx_rows, row_start):
    n = xs.shape[0]
    tm = ROW_TILE
    t0 = row_start // tm

    def zc(c):
        return pl.BlockSpec((None, tm, PROJ_CHUNK), lambda i: (c, i + t0, 0))

    row = lambda w: pl.BlockSpec((tm, w), lambda i: (i + t0, 0))
    return pl.pallas_call(
        _merge_kernel,
        grid=(n // tm - t0,),
        in_specs=[
            row(D_MODEL),
            pl.BlockSpec((None, 3, BATCH, D_MODEL),
                         lambda i: (((i + t0) * tm >= ctx_rows).astype(jnp.int32), 1, 0, 0)),
            pl.BlockSpec((tm // BATCH, BATCH * D_MODEL), lambda i: (i + t0, 0)),
            row(D_MODEL), row(D_MODEL),
            zc(0), zc(1), zc(4), zc(5), zc(6), zc(7), zc(8), zc(9),
            _resident((D_MODEL, D_MODEL)), _resident((D_MODEL, D_MODEL)), _resident((D_MODEL, D_MODEL)),
        ],
        out_specs=pl.BlockSpec((tm, D_MODEL), lambda i: (i, 0)),
        out_shape=jax.ShapeDtypeStruct((n - row_start, D_MODEL), F32),
        scratch_shapes=[pltpu.VMEM((D_MODEL // LANES, tm, LANES), F32)],
        compiler_params=_cparams("parallel"),
        name="merge_out_proj",
    )(xs, tab, o_seq, hf, hb, z, z, z, z, z, z, z, z, w_ret_o, w_lru_o, w_out)


def _final_kernel(x_ref, g_ref, o_ref, slab_scr):
    x = x_ref[...]
    steps = x.shape[0] // BATCH
    ms = jnp.mean(x * x, axis=-1, keepdims=True)
    y = (x * lax.rsqrt(ms + EPS)) * g_ref[...]
    n_slabs = slab_scr.shape[0]
    for s in range(n_slabs):
        slab_scr[s] = y[:, s * LANES:(s + 1) * LANES]
    for b in range(BATCH):
        for s in range(n_slabs):
            o_ref[b, :, s * LANES:(s + 1) * LANES] = slab_scr[s, pl.ds(b, steps, stride=BATCH), :]


def _final_norm(xs, g):
    n = xs.shape[0]
    tm = ROW_TILE
    return pl.pallas_call(
        _final_kernel,
        grid=(n // tm,),
        in_specs=[
            pl.BlockSpec((tm, D_MODEL), lambda i: (i, 0)),
            pl.BlockSpec((1, D_MODEL), lambda i: (0, 0)),
        ],
        out_specs=pl.BlockSpec((BATCH, tm // BATCH, D_MODEL), lambda i: (0, i, 0)),
        out_shape=jax.ShapeDtypeStruct((BATCH, n // BATCH, D_MODEL), F32),
        scratch_shapes=[pltpu.VMEM((D_MODEL // LANES, tm, LANES), F32)],
        compiler_params=_cparams("parallel"),
        name="final_norm",
    )(xs, g)


def _time_major_kernel(ctx_ref, x_ref, o_ref, slab_scr, *, n_ctx_tiles):
    steps = o_ref.shape[0] // BATCH
    n_slabs = slab_scr.shape[0]

    def emit(src_ref):
        for b in range(BATCH):
            for s in range(n_slabs):
                slab_scr[s, pl.ds(b, steps, stride=BATCH), :] = src_ref[b, :, s * LANES:(s + 1) * LANES]
        o_ref[...] = jnp.concatenate([slab_scr[s] for s in range(n_slabs)], axis=1)

    @pl.when(pl.program_id(0) < n_ctx_tiles)
    def _():
        emit(ctx_ref)

    @pl.when(pl.program_id(0) >= n_ctx_tiles)
    def _():
        emit(x_ref)


def _to_time_major(ctx, x):
    t_ctx, t_lat = ctx.shape[1], x.shape[1]
    tm = ROW_TILE
    steps = tm // BATCH
    n_ctx_tiles = t_ctx // steps
    n = (t_ctx + t_lat) * BATCH
    return pl.pallas_call(
        functools.partial(_time_major_kernel, n_ctx_tiles=n_ctx_tiles),
        grid=(n // tm,),
        in_specs=[
            pl.BlockSpec((BATCH, steps, D_MODEL), lambda i: (0, jnp.minimum(i, n_ctx_tiles - 1), 0)),
            pl.BlockSpec((BATCH, steps, D_MODEL), lambda i: (0, jnp.maximum(i - n_ctx_tiles, 0), 0)),
        ],
        out_specs=pl.BlockSpec((tm, D_MODEL), lambda i: (i, 0)),
        out_shape=jax.ShapeDtypeStruct((n, D_MODEL), F32),
        scratch_shapes=[pltpu.VMEM((D_MODEL // LANES, tm, LANES), F32)],
        compiler_params=_cparams("parallel"),
        name="to_time_major",
    )(ctx, x)


def _rotary_tables(t_ctx, t_lat):
    pos = jnp.arange(t_lat, dtype=jnp.int32)
    row = (pos // GRID_W).astype(F32)
    col = (pos % GRID_W).astype(F32)
    n_f = RET_DK // 4
    inv = ROPE_BASE ** (-jnp.arange(n_f, dtype=F32) / n_f)
    ang = jnp.concatenate([row[:, None] * inv, col[:, None] * inv], axis=-1)
    cos = jnp.concatenate([jnp.ones((t_ctx, RET_DK // 2), F32), jnp.cos(ang)], axis=0)
    sin = jnp.concatenate([jnp.zeros((t_ctx, RET_DK // 2), F32), jnp.sin(ang)], axis=0)
    cos128 = jnp.tile(cos, (1, 4))
    sin128 = jnp.tile(jnp.concatenate([-sin, sin], axis=1), (1, 2))
    return jnp.repeat(cos128, BATCH, axis=0), jnp.repeat(sin128, BATCH, axis=0)


def _lru_gate_weights(gate_w, gate_b):
    n_cc = D_MODEL // LRU_GW
    per = LRU_GW // LRU_BW
    gate_w = 0.5 * gate_w
    gate_b = 0.5 * gate_b
    w = gate_w.reshape(2, 2, n_cc, per, LRU_BW, LRU_BW)
    eye = jnp.eye(per, dtype=gate_w.dtype)
    bd = jnp.einsum('dgcpij,pq->dgcpiqj', w, eye).reshape(2, 2, n_cc, LRU_GW, LRU_GW)
    wcat = jnp.concatenate([bd[:, 0], bd[:, 1]], axis=-1).astype(BF16)
    b = gate_b.reshape(2, 2, n_cc, 1, LRU_GW)
    bcat = jnp.concatenate([b[:, 0], b[:, 1]], axis=-1)
    return wcat, bcat


def kernel(x, c, ctx, c_ctx, w_mod, b_mod, norm_g, ffn1_w_gu, ffn1_w_down, ffn2_w_gu, ffn2_w_down,
           w_in, ret_decay_logit, w_ret_o, lru_conv_w, lru_conv_b, lru_gate_w, lru_gate_b,
           lru_lambda, w_lru_o, w_out, final_g):
    depth = w_mod.shape[0]
    t_lat, t_ctx = x.shape[1], ctx.shape[1]
    t_all = t_lat + t_ctx
    n_rows = t_all * BATCH
    ctx_rows = t_ctx * BATCH
    assert x.shape[0] == BATCH and x.shape[2] == D_MODEL
    assert t_ctx % LRU_TB == 0 and t_lat % LRU_TB == 0 and ctx_rows % ROW_TILE == 0

    xs = _to_time_major(ctx, x)
    tabs = _mod_tables(c, c_ctx, w_mod, b_mod)
    cos_t, sin_t = _rotary_tables(t_ctx, t_lat)
    n_gw = D_MODEL // LRU_GW

    for l in range(depth):
        last = l == depth - 1
        tab = tabs[l]
        g = norm_g[l].reshape(3, 1, D_MODEL)
        xs = _ffn_sublayer(xs, tab, 0, g[0], ffn1_w_gu, ffn1_w_down, l, ctx_rows)
        q, k, v, z = _in_projection(xs, tab, g[1], cos_t, sin_t, w_in, l, ctx_rows)
        logit_rows = jnp.broadcast_to(ret_decay_logit[l].astype(F32)[:, :, None, None],
                                      (2, RET_HEADS, 1, LANES))
        o_seq = _retention(q, k, v, logit_rows, t_ctx // RET_CHUNK)
        gw, gb = _lru_gate_weights(lru_gate_w[l], lru_gate_b[l])
        hf, hb = _rg_lru(z, lru_conv_w[l], lru_conv_b[l].reshape(1, D_MODEL), gw, gb,
                         lru_lambda[l].reshape(2, n_gw, 1, LRU_GW), n_rows, t_ctx // LRU_TB)
        row_start = ctx_rows if last else 0
        xs = _merge(xs, tab, o_seq, hf, hb, z, w_ret_o[l].astype(BF16), w_lru_o[l].astype(BF16),
                    w_out[l].astype(BF16), ctx_rows, row_start)
        xs = _ffn_sublayer(xs, tab, 6, g[2], ffn2_w_gu, ffn2_w_down, l, ctx_rows - row_start)
    return _final_norm(xs, final_g.reshape(1, D_MODEL))
```

```python
import functools

import jax
import jax.numpy as jnp
from jax import lax
from jax.experimental import pallas as pl
from jax.experimental.pallas import tpu as pltpu

F32 = jnp.float32
BF16 = jnp.bfloat16

D_MODEL = 1024
BATCH = 8
LANES = 128
RET_HEADS = 8
RET_DK = 64
RET_DV = 128
RET_CHUNK = 128
HEAD_GROUP = 8
GRID_W = 64
ROPE_BASE = 10000.0
LRU_BLOCKS = 16
LRU_BW = D_MODEL // LRU_BLOCKS
LRU_C = 8.0
LRU_GW = 256
LRU_CW = 512
LRU_TB = 128
CONV_W = 4
FFN_HIDDEN = 2816
FFN_TF = 256
FFN_RES = 0.5
N_MOD = 9
EPS = 1e-6
PROJ_CHUNK = 512
N_PROJ_CHUNKS = 14
ROW_TILE = 512
VMEM_LIMIT = 52 * 1024 * 1024


def _cparams(*sem):
    return pltpu.CompilerParams(dimension_semantics=sem, vmem_limit_bytes=VMEM_LIMIT)


def _resident(shape):
    return pl.BlockSpec(shape, lambda *_: (0,) * len(shape), pipeline_mode=pl.Buffered(1))


def _sigmoid(x):
    return 1.0 / (1.0 + jnp.exp(-x))


def _ada_norm(x, g, shift, scale):
    rows = x.shape[0]
    ms = jnp.mean(x * x, axis=-1, keepdims=True)
    y = (x * lax.rsqrt(ms + EPS)) * g
    y3 = y.reshape(rows // BATCH, BATCH, D_MODEL)
    h = y3 * (1.0 + scale)[None] + shift[None]
    return h.reshape(rows, D_MODEL)


def _mod_kernel(c_ref, w_ref, b_ref, o_ref):
    c = c_ref[...]
    s = (c * _sigmoid(c)).astype(BF16)
    o_ref[...] = jnp.dot(s, w_ref[...].astype(BF16), preferred_element_type=F32) + b_ref[...]


def _mod_tables(c, c_ctx, w_mod, b_mod):
    depth = w_mod.shape[0]
    cc = jnp.zeros((2 * BATCH, D_MODEL), F32).at[:BATCH].set(c).at[BATCH].set(c_ctx)
    out = pl.pallas_call(
        _mod_kernel,
        grid=(depth, N_MOD),
        in_specs=[
            pl.BlockSpec((2 * BATCH, D_MODEL), lambda l, j: (0, 0)),
            pl.BlockSpec((None, D_MODEL, D_MODEL), lambda l, j: (l, 0, j)),
            pl.BlockSpec((None, 1, D_MODEL), lambda l, j: (l, 0, j)),
        ],
        out_specs=pl.BlockSpec((None, 2 * BATCH, D_MODEL), lambda l, j: (l, 0, j)),
        out_shape=jax.ShapeDtypeStruct((depth, 2 * BATCH, N_MOD * D_MODEL), F32),
        compiler_params=_cparams("parallel", "parallel"),
        name="adaln_mod",
    )(cc, w_mod, b_mod.reshape(depth, 1, N_MOD * D_MODEL))
    out = out.reshape(depth, 2 * BATCH, N_MOD, D_MODEL)
    lat = out[:, :BATCH].transpose(0, 2, 1, 3)
    ctx = jnp.broadcast_to(out[:, BATCH][:, :, None, :], lat.shape)
    return jnp.stack([ctx, lat], axis=1)


def _ffn_kernel(x_ref, mod_ref, g_ref, wg32_ref, wu32_ref, wd32_ref, o_ref,
                wg_s, wu_s, wd_s, h_scr, act_scr, acc_scr):
    nf = FFN_HIDDEN // FFN_TF
    step = pl.program_id(0)

    @pl.when(step < nf)
    def _():
        wg_s[step] = wg32_ref[...].astype(BF16)
        wu_s[step] = wu32_ref[...].astype(BF16)
        wd_s[step] = wd32_ref[...].astype(BF16)

    @pl.when(step >= nf)
    def _():
        rows = x_ref.shape[0]
        x = x_ref[...]
        h_scr[...] = _ada_norm(x, g_ref[...], mod_ref[0], mod_ref[1]).astype(BF16)
        acc_scr[...] = jnp.zeros_like(acc_scr)

        def gated(k):
            hb = h_scr[...]
            u = jnp.dot(hb, wg_s[k], preferred_element_type=F32)
            v = jnp.dot(hb, wu_s[k], preferred_element_type=F32)
            return ((u * _sigmoid(u)) * v).astype(BF16)

        def down(k):
            return jnp.dot(act_scr[...], wd_s[k], preferred_element_type=F32)

        act_scr[...] = gated(0)
        for k in range(1, nf):
            part = down(k - 1)
            act_scr[...] = gated(k)
            acc_scr[...] += part
        acc_scr[...] += down(nf - 1)
        y3 = acc_scr[...].reshape(rows // BATCH, BATCH, D_MODEL)
        x3 = x.reshape(rows // BATCH, BATCH, D_MODEL)
        o_ref[...] = (x3 + (FFN_RES * mod_ref[2])[None] * y3).reshape(rows, D_MODEL)


def _ffn_sublayer(xs, tab, sub, g, w_gu, w_down, layer, ctx_rows):
    n = xs.shape[0]
    tm = ROW_TILE
    nf = FFN_HIDDEN // FFN_TF
    tile = lambda i: jnp.maximum(i - nf, 0)
    chunk = lambda i: jnp.minimum(i, nf - 1)
    return pl.pallas_call(
        _ffn_kernel,
        grid=(nf + n // tm,),
        in_specs=[
            pl.BlockSpec((tm, D_MODEL), lambda i: (tile(i), 0)),
            pl.BlockSpec((None, 3, BATCH, D_MODEL),
                         lambda i: ((tile(i) * tm >= ctx_rows).astype(jnp.int32), sub // 3, 0, 0)),
            _resident((1, D_MODEL)),
            pl.BlockSpec((None, D_MODEL, FFN_TF), lambda i: (layer, 0, chunk(i))),
            pl.BlockSpec((None, D_MODEL, FFN_TF), lambda i: (layer, 0, nf + chunk(i))),
            pl.BlockSpec((None, FFN_TF, D_MODEL), lambda i: (layer, chunk(i), 0)),
        ],
        out_specs=pl.BlockSpec((tm, D_MODEL), lambda i: (tile(i), 0)),
        out_shape=jax.ShapeDtypeStruct((n, D_MODEL), F32),
        scratch_shapes=[pltpu.VMEM((nf, D_MODEL, FFN_TF), BF16), pltpu.VMEM((nf, D_MODEL, FFN_TF), BF16),
                        pltpu.VMEM((nf, FFN_TF, D_MODEL), BF16),
                        pltpu.VMEM((tm, D_MODEL), BF16), pltpu.VMEM((tm, FFN_TF), BF16),
                        pltpu.VMEM((tm, D_MODEL), F32)],
        compiler_params=_cparams("arbitrary"),
        name="ffn_sublayer",
    )(xs, tab, g, w_gu, w_gu, w_down)


def _swap_halves(a):
    w = a.shape[1]
    lane = lax.broadcasted_iota(jnp.int32, a.shape, 1)
    first_half = (lane % RET_DK) < (RET_DK // 2)
    return jnp.where(first_half, pltpu.roll(a, w - RET_DK // 2, 1), pltpu.roll(a, RET_DK // 2, 1))


def _rows_to_sequences(a, slab_scr, out_ref):
    rows, width = a.shape
    steps = rows // BATCH
    n_slabs = width // LANES
    for s in range(n_slabs):
        slab_scr[s] = a[:, s * LANES:(s + 1) * LANES]
    for b in range(BATCH):
        for s in range(n_slabs):
            col = b * width + s * LANES
            out_ref[:, col:col + LANES] = slab_scr[s, pl.ds(b, steps, stride=BATCH), :].astype(out_ref.dtype)


def _inproj_kernel(x_ref, mod_ref, g_ref, cos_ref, sin_ref, w32_ref, q_ref, k_ref, v_ref, z_ref,
                   w_s, h_scr, slab_scr):
    step = pl.program_id(0)

    @pl.when(step < N_PROJ_CHUNKS)
    def _():
        w_s[step] = w32_ref[...].astype(BF16)

    @pl.when(step >= N_PROJ_CHUNKS)
    def _():
        h_scr[...] = _ada_norm(x_ref[...], g_ref[...], mod_ref[0], mod_ref[1]).astype(BF16)
        reps = PROJ_CHUNK // cos_ref.shape[1]
        cos = jnp.tile(cos_ref[...], (1, reps))
        sin = jnp.tile(sin_ref[...], (1, reps))

        def rotate(a):
            return a * cos + _swap_halves(a) * sin

        def proj(c):
            return jnp.dot(h_scr[...], w_s[c], preferred_element_type=F32)

        _rows_to_sequences(rotate(proj(0)), slab_scr, q_ref)
        _rows_to_sequences(rotate(proj(1)) * (RET_DK ** -0.5), slab_scr, k_ref)
        for c in range(2):
            _rows_to_sequences(proj(2 + c), slab_scr, v_ref.at[c])
        for c in range(4, N_PROJ_CHUNKS):
            z_ref[c - 4] = proj(c).astype(BF16)


def _in_projection(xs, tab, g, cos_t, sin_t, w_in, layer, ctx_rows):
    n = xs.shape[0]
    tm = ROW_TILE
    nz = N_PROJ_CHUNKS - 4
    steps = tm // BATCH
    t_all = n // BATCH
    seq_w = BATCH * PROJ_CHUNK
    nw = N_PROJ_CHUNKS
    tile = lambda i: jnp.maximum(i - nw, 0)
    return pl.pallas_call(
        _inproj_kernel,
        grid=(nw + n // tm,),
        in_specs=[
            pl.BlockSpec((tm, D_MODEL), lambda i: (tile(i), 0)),
            pl.BlockSpec((None, 3, BATCH, D_MODEL),
                         lambda i: ((tile(i) * tm >= ctx_rows).astype(jnp.int32), 1, 0, 0)),
            _resident((1, D_MODEL)),
            pl.BlockSpec((tm, LANES), lambda i: (tile(i), 0)),
            pl.BlockSpec((tm, LANES), lambda i: (tile(i), 0)),
            pl.BlockSpec((None, D_MODEL, PROJ_CHUNK), lambda i: (layer, 0, jnp.minimum(i, nw - 1))),
        ],
        out_specs=[
            pl.BlockSpec((steps, seq_w), lambda i: (tile(i), 0)),
            pl.BlockSpec((steps, seq_w), lambda i: (tile(i), 0)),
            pl.BlockSpec((2, steps, seq_w), lambda i: (0, tile(i), 0)),
            pl.BlockSpec((nz, tm, PROJ_CHUNK), lambda i: (0, tile(i), 0)),
        ],
        out_shape=[
            jax.ShapeDtypeStruct((t_all, seq_w), BF16),
            jax.ShapeDtypeStruct((t_all, seq_w), BF16),
            jax.ShapeDtypeStruct((2, t_all, seq_w), BF16),
            jax.ShapeDtypeStruct((nz, n, PROJ_CHUNK), BF16),
        ],
        scratch_shapes=[pltpu.VMEM((N_PROJ_CHUNKS, D_MODEL, PROJ_CHUNK), BF16),
                        pltpu.VMEM((tm, D_MODEL), BF16),
                        pltpu.VMEM((PROJ_CHUNK // LANES, tm, LANES), F32)],
        compiler_params=_cparams("arbitrary"),
        name="in_projection",
    )(xs, tab, g, cos_t, sin_t, w_in)


def _log_sigmoid(x):
    return jnp.minimum(x, 0.0) - jnp.log1p(jnp.exp(-jnp.abs(x)))


def _dot_t1(a, b):
    return lax.dot_general(a, b, (((1,), (1,)), ((), ())), preferred_element_type=F32)


def _retention_kernel(lg_ref, q_ref, k_ref, v_ref, o_ref,
                      dmat, qdec, kdec, cdec, sb_scr, sf_scr, *, n_ctx_chunks):
    c_len = RET_CHUNK
    pair_k = 2 * RET_DK
    pair_v = 2 * RET_DV
    n_pairs = HEAD_GROUP // 2
    n_chunks = q_ref.shape[0] // c_len
    hg = pl.program_id(1)
    row_i = lax.broadcasted_iota(jnp.int32, (c_len, c_len), 0)
    col_i = lax.broadcasted_iota(jnp.int32, (c_len, c_len), 1)
    row = row_i.astype(F32)
    col = col_i.astype(F32)
    rel = row - col
    first_lanes = col_i < RET_DK
    first_rows = row_i < RET_DK
    own_block = (lax.broadcasted_iota(jnp.int32, (pair_k, pair_v), 0) < RET_DK) == (
        lax.broadcasted_iota(jnp.int32, (pair_k, pair_v), 1) < RET_DV)

    for p in range(n_pairs):
        h0 = hg * HEAD_GROUP + 2 * p
        lg = [[_log_sigmoid(lg_ref[d, h0 + e]) for e in range(2)] for d in range(2)]
        for e in range(2):
            dmat[2 * p + e] = jnp.where(rel >= 0.0, jnp.exp(lg[0][e] * jnp.maximum(rel, 0.0)),
                                        jnp.exp(lg[1][e] * jnp.maximum(-rel, 0.0)))
        lane_f = jnp.where(first_lanes, lg[0][0], lg[0][1])
        lane_b = jnp.where(first_lanes, lg[1][0], lg[1][1])
        row_f = jnp.where(first_rows, lg[0][0], lg[0][1])
        row_b = jnp.where(first_rows, lg[1][0], lg[1][1])
        qdec[0, p] = jnp.exp(lane_f * (row + 1.0))
        qdec[1, p] = jnp.exp(lane_b * (c_len - row))
        kdec[0, p] = jnp.exp(row_f * (c_len - 1.0 - col))
        kdec[1, p] = jnp.exp(row_b * col)
        cdec[0, p] = jnp.exp(row_f * c_len)
        cdec[1, p] = jnp.exp(row_b * c_len)

    def advance(d, p, s, kp, vp):
        kd = (kp.astype(F32).T * kdec[d, p]).astype(BF16)
        upd = jnp.dot(kd, vp, preferred_element_type=F32)
        cd = cdec[d, p]
        return s * jnp.concatenate([cd, cd], axis=1) + jnp.where(own_block, upd, 0.0)

    sf_scr[...] = jnp.zeros_like(sf_scr)

    def bwd_body(i, carry):
        c = jnp.where(i < n_ctx_chunks, n_ctx_chunks - 1 - i, n_chunks - 1 + n_ctx_chunks - i)
        rows = pl.ds(pl.multiple_of(c * c_len, c_len), c_len)
        for p in range(n_pairs):
            kp = k_ref[rows, p * pair_k:(p + 1) * pair_k]
            vp = v_ref[(p * pair_v) // PROJ_CHUNK, rows, pl.ds((p * pair_v) % PROJ_CHUNK, pair_v)]
            s = sf_scr[p]
            sb_scr[c, p] = s.astype(BF16)
            sf_scr[p] = advance(1, p, s, kp, vp)
        return carry

    lax.fori_loop(0, n_chunks, bwd_body, 0, unroll=3)

    sf_scr[...] = jnp.zeros_like(sf_scr)
    zero_v = jnp.zeros((c_len, RET_DV), BF16)
    zero_q = jnp.zeros((c_len, pair_k), BF16)

    def fwd_body(c, carry):
        rows = pl.ds(pl.multiple_of(c * c_len, c_len), c_len)
        for p in range(n_pairs):
            qp = q_ref[rows, p * pair_k:(p + 1) * pair_k]
            kp = k_ref[rows, p * pair_k:(p + 1) * pair_k]
            vp = v_ref[(p * pair_v) // PROJ_CHUNK, rows, pl.ds((p * pair_v) % PROJ_CHUNK, pair_v)]
            q_split = jnp.concatenate([jnp.where(first_lanes, qp, zero_q),
                                       jnp.where(first_lanes, zero_q, qp)], axis=0)
            scores = _dot_t1(q_split, kp)
            pa = (scores[:c_len] * dmat[2 * p]).astype(BF16)
            pb = (scores[c_len:] * dmat[2 * p + 1]).astype(BF16)
            qf = qp.astype(F32)
            s = sf_scr[p]
            lhs = jnp.concatenate([pa, pb, (qf * qdec[0, p]).astype(BF16), (qf * qdec[1, p]).astype(BF16)],
                                  axis=1)
            rhs = jnp.concatenate([
                jnp.concatenate([vp[:, :RET_DV], zero_v], axis=1),
                jnp.concatenate([zero_v, vp[:, RET_DV:]], axis=1),
                s.astype(BF16), sb_scr[c, p]], axis=0)
            o = jnp.dot(lhs, rhs, preferred_element_type=F32)
            sf_scr[p] = advance(0, p, s, kp, vp)
            for e in range(2):
                oe = o[:, e * RET_DV:(e + 1) * RET_DV]
                mu = jnp.mean(oe, axis=-1, keepdims=True)
                d = oe - mu
                var = jnp.mean(d * d, axis=-1, keepdims=True)
                col0 = p * pair_v + e * RET_DV
                o_ref[rows, col0:col0 + RET_DV] = (d * lax.rsqrt(var + EPS)).astype(BF16)
        return carry

    lax.fori_loop(0, n_chunks, fwd_body, 0, unroll=2)


def _retention(q, k, v, logit_rows, n_ctx_chunks):
    t_all = q.shape[0]
    n_groups = RET_HEADS // HEAD_GROUP
    n_pairs = HEAD_GROUP // 2
    qw = HEAD_GROUP * RET_DK
    vw = HEAD_GROUP * RET_DV
    v_chunks = vw // PROJ_CHUNK
    assert vw % PROJ_CHUNK == 0 and n_groups * v_chunks == v.shape[0]
    n_chunks = t_all // RET_CHUNK
    return pl.pallas_call(
        functools.partial(_retention_kernel, n_ctx_chunks=n_ctx_chunks),
        grid=(BATCH, n_groups),
        in_specs=[
            _resident((2, RET_HEADS, 1, LANES)),
            pl.BlockSpec((t_all, qw), lambda b, h: (0, b * n_groups + h)),
            pl.BlockSpec((t_all, qw), lambda b, h: (0, b * n_groups + h)),
            pl.BlockSpec((v_chunks, t_all, PROJ_CHUNK), lambda b, h: (h, 0, b)),
        ],
        out_specs=pl.BlockSpec((t_all, vw), lambda b, h: (0, b * n_groups + h)),
        out_shape=jax.ShapeDtypeStruct((t_all, BATCH * RET_HEADS * RET_DV), BF16),
        scratch_shapes=[
            pltpu.VMEM((HEAD_GROUP, RET_CHUNK, RET_CHUNK), F32),
            pltpu.VMEM((2, n_pairs, RET_CHUNK, 2 * RET_DK), F32),
            pltpu.VMEM((2, n_pairs, 2 * RET_DK, RET_CHUNK), F32),
            pltpu.VMEM((2, n_pairs, 2 * RET_DK, RET_CHUNK), F32),
            pltpu.VMEM((n_chunks, n_pairs, 2 * RET_DK, 2 * RET_DV), BF16),
            pltpu.VMEM((n_pairs, 2 * RET_DK, 2 * RET_DV), F32),
        ],
        compiler_params=_cparams("parallel", "parallel"),
        name="retention",
    )(logit_rows, q, k, v)


def _lru_kernel(xf_ref, xfp_ref, xfn_ref, xb_ref, xbp_ref, xbn_ref,
                cw_ref, cb_ref, gw_ref, gb_ref, lam_ref, hf_ref, hb_ref,
                a_scr, b_scr, h_scr, *, n_ctx_blocks, n_blocks):
    j = pl.program_id(1)
    jb = jnp.where(j < n_ctx_blocks, n_ctx_blocks - 1 - j, n_blocks - 1 + n_ctx_blocks - j)
    rows = xf_ref.shape[0]
    halo = xfp_ref.shape[0]
    cw = cw_ref[...]
    cb = cb_ref[...]

    def prepare(d, blk, x_ref, xp_ref, xn_ref):
        is_first = jnp.logical_or(blk == 0, blk == n_ctx_blocks)
        is_last = jnp.logical_or(blk == n_ctx_blocks - 1, blk == n_blocks - 1)
        prev = jnp.where(is_first, 0.0, xp_ref[...].astype(F32))
        nxt = jnp.where(is_last, 0.0, xn_ref[...].astype(F32))
        xe = jnp.concatenate([prev, x_ref[...].astype(F32), nxt], axis=0)
        u = cb
        for tap in range(CONV_W):
            off = halo - (2 - tap) * BATCH
            u = u + xe[off:off + rows] * cw[tap:tap + 1]
        for grp in range(LRU_CW // LRU_GW):
            cols = slice(grp * LRU_GW, (grp + 1) * LRU_GW)
            ug = u[:, cols]
            t = jnp.tanh(jnp.dot(ug.astype(BF16), gw_ref[d, grp], preferred_element_type=F32)
                         + gb_ref[d, grp])
            lam = lam_ref[d, grp]
            softplus = jnp.maximum(-lam, 0.0) + jnp.log1p(jnp.exp(-jnp.abs(lam)))
            half_c = (-0.5 * LRU_C) * softplus
            log_a = half_c * t[:, :LRU_GW] + half_c
            half_u = 0.5 * ug
            iu = half_u * t[:, LRU_GW:] + half_u
            th = jnp.tanh(log_a)
            w = (-2.0 * th) / (1.0 - th)
            a_scr[d, :, cols] = jnp.exp(log_a)
            b_scr[d, :, cols] = jnp.where(w > 0.0, w * lax.rsqrt(w), 0.0) * iu

    prepare(0, j, xf_ref, xfp_ref, xfn_ref)
    prepare(1, jb, xb_ref, xbp_ref, xbn_ref)

    @pl.when(j == 0)
    def _():
        h_scr[...] = jnp.zeros_like(h_scr)

    n_steps = rows // BATCH

    def step(s, carry):
        hf, hb = carry
        rf = pl.ds(pl.multiple_of(s * BATCH, BATCH), BATCH)
        hf = a_scr[0, rf, :] * hf + b_scr[0, rf, :]
        hf_ref[rf, :] = hf
        rb = pl.ds(pl.multiple_of((n_steps - 1 - s) * BATCH, BATCH), BATCH)
        hb = a_scr[1, rb, :] * hb + b_scr[1, rb, :]
        hb_ref[rb, :] = hb
        return hf, hb

    hf, hb = lax.fori_loop(0, n_steps, step, (h_scr[0], h_scr[1]), unroll=8)
    h_scr[0] = hf
    h_scr[1] = hb


def _rg_lru(z, conv_w, conv_b, gate_w, gate_b, lam, n_rows, n_ctx_blocks):
    rows = LRU_TB * BATCH
    n_blocks = n_rows // rows
    n_cc = D_MODEL // LRU_CW
    n_grp = LRU_CW // LRU_GW
    per_chunk = PROJ_CHUNK // LRU_CW
    halo = 16
    hb_per_block = rows // halo
    n_halo = n_rows // halo

    def bwd_block(j):
        return jnp.where(j < n_ctx_blocks, n_ctx_blocks - 1 - j, n_blocks - 1 + n_ctx_blocks - j)

    def cur(f):
        return pl.BlockSpec((None, rows, LRU_CW), lambda c, j: (2 + c // per_chunk, f(j), c % per_chunk))

    def prev(f):
        return pl.BlockSpec((None, halo, LRU_CW),
                            lambda c, j: (2 + c // per_chunk, jnp.maximum(f(j) * hb_per_block - 1, 0),
                                          c % per_chunk))

    def nxt(f):
        return pl.BlockSpec((None, halo, LRU_CW),
                            lambda c, j: (2 + c // per_chunk,
                                          jnp.minimum((f(j) + 1) * hb_per_block, n_halo - 1),
                                          c % per_chunk))

    ident = lambda j: j
    return pl.pallas_call(
        functools.partial(_lru_kernel, n_ctx_blocks=n_ctx_blocks, n_blocks=n_blocks),
        grid=(n_cc, n_blocks),
        in_specs=[
            cur(ident), prev(ident), nxt(ident), cur(bwd_block), prev(bwd_block), nxt(bwd_block),
            pl.BlockSpec((CONV_W, LRU_CW), lambda c, j: (0, c)),
            pl.BlockSpec((1, LRU_CW), lambda c, j: (0, c)),
            pl.BlockSpec((2, n_grp, LRU_GW, 2 * LRU_GW), lambda c, j: (0, c, 0, 0)),
            pl.BlockSpec((2, n_grp, 1, 2 * LRU_GW), lambda c, j: (0, c, 0, 0)),
            pl.BlockSpec((2, n_grp, 1, LRU_GW), lambda c, j: (0, c, 0, 0)),
        ],
        out_specs=[
            pl.BlockSpec((rows, LRU_CW), lambda c, j: (j, c)),
            pl.BlockSpec((rows, LRU_CW), lambda c, j: (bwd_block(j), c)),
        ],
        out_shape=[jax.ShapeDtypeStruct((n_rows, D_MODEL), F32)] * 2,
        scratch_shapes=[
            pltpu.VMEM((2, rows, LRU_CW), F32),
            pltpu.VMEM((2, rows, LRU_CW), F32),
            pltpu.VMEM((2, BATCH, LRU_CW), F32),
        ],
        compiler_params=_cparams("parallel", "arbitrary"),
        name="rg_lru",
    )(z, z, z, z, z, z, conv_w, conv_b, gate_w, gate_b, lam)


def _gelu_tanh(x):
    return 0.5 * x * (1.0 + jnp.tanh(0.7978845608028654 * (x + 0.044715 * (x * x * x))))


def _sequences_to_rows(o_ref, slab_scr):
    steps = o_ref.shape[0]
    n_slabs = slab_scr.shape[0]
    width = n_slabs * LANES
    for b in range(BATCH):
        for s in range(n_slabs):
            col = b * width + s * LANES
            slab_scr[s, pl.ds(b, steps, stride=BATCH), :] = o_ref[:, col:col + LANES].astype(F32)


def _merge_kernel(x_ref, mod_ref, o_ref_in, hf_ref, hb_ref, gr0_ref, gr1_ref, gl0_ref, gl1_ref,
                  ga0_ref, ga1_ref, gb0_ref, gb1_ref, wr_ref, wl_ref, wo_ref, o_ref, slab_scr):
    rows = x_ref.shape[0]
    n_slabs = slab_scr.shape[0]
    _sequences_to_rows(o_ref_in, slab_scr)

    def both(r0, r1):
        return jnp.concatenate([r0[...], r1[...]], axis=1).astype(F32)

    gr = both(gr0_ref, gr1_ref)
    o_ret = jnp.concatenate([slab_scr[s] for s in range(n_slabs)], axis=1) * (gr * _sigmoid(gr))
    y_a = jnp.dot(o_ret.astype(BF16), wr_ref[...], preferred_element_type=F32)
    h = hf_ref[...] + hb_ref[...]
    y_b = jnp.dot((h * _gelu_tanh(both(gl0_ref, gl1_ref))).astype(BF16), wl_ref[...],
                  preferred_element_type=F32)
    m = (_sigmoid(both(ga0_ref, ga1_ref)) * y_a + _sigmoid(both(gb0_ref, gb1_ref)) * y_b).astype(BF16)
    y = jnp.dot(m, wo_ref[...], preferred_element_type=F32)
    y3 = y.reshape(rows // BATCH, BATCH, D_MODEL)
    x3 = x_ref[...].reshape(rows // BATCH, BATCH, D_MODEL)
    o_ref[...] = (x3 + mod_ref[2][None] * y3).reshape(rows, D_MODEL)


def _merge(xs, tab, o_seq, hf, hb, z, w_ret_o, w_lru_o, w_out, ctx_rows, row_start):
    n = xs.shape[0]
    tm = ROW_TILE
    t0 = row_start // tm

    def zc(c):
        return pl.BlockSpec((None, tm, PROJ_CHUNK), lambda i: (c, i + t0, 0))

    row = lambda w: pl.BlockSpec((tm, w), lambda i: (i + t0, 0))
    return pl.pallas_call(
        _merge_kernel,
        grid=(n // tm - t0,),
        in_specs=[
            row(D_MODEL),
            pl.BlockSpec((None, 3, BATCH, D_MODEL),
                         lambda i: (((i + t0) * tm >= ctx_rows).astype(jnp.int32), 1, 0, 0)),
            pl.BlockSpec((tm // BATCH, BATCH * D_MODEL), lambda i: (i + t0, 0)),
            row(D_MODEL), row(D_MODEL),
            zc(0), zc(1), zc(4), zc(5), zc(6), zc(7), zc(8), zc(9),
            _resident((D_MODEL, D_MODEL)), _resident((D_MODEL, D_MODEL)), _resident((D_MODEL, D_MODEL)),
        ],
        out_specs=pl.BlockSpec((tm, D_MODEL), lambda i: (i, 0)),
        out_shape=jax.ShapeDtypeStruct((n - row_start, D_MODEL), F32),
        scratch_shapes=[pltpu.VMEM((D_MODEL // LANES, tm, LANES), F32)],
        compiler_params=_cparams("parallel"),
        name="merge_out_proj",
    )(xs, tab, o_seq, hf, hb, z, z, z, z, z, z, z, z, w_ret_o, w_lru_o, w_out)


def _final_kernel(x_ref, g_ref, o_ref, slab_scr):
    x = x_ref[...]
    steps = x.shape[0] // BATCH
    ms = jnp.mean(x * x, axis=-1, keepdims=True)
    y = (x * lax.rsqrt(ms + EPS)) * g_ref[...]
    n_slabs = slab_scr.shape[0]
    for s in range(n_slabs):
        slab_scr[s] = y[:, s * LANES:(s + 1) * LANES]
    for b in range(BATCH):
        for s in range(n_slabs):
            o_ref[b, :, s * LANES:(s + 1) * LANES] = slab_scr[s, pl.ds(b, steps, stride=BATCH), :]


def _final_norm(xs, g):
    n = xs.shape[0]
    tm = ROW_TILE
    return pl.pallas_call(
        _final_kernel,
        grid=(n // tm,),
        in_specs=[
            pl.BlockSpec((tm, D_MODEL), lambda i: (i, 0)),
            pl.BlockSpec((1, D_MODEL), lambda i: (0, 0)),
        ],
        out_specs=pl.BlockSpec((BATCH, tm // BATCH, D_MODEL), lambda i: (0, i, 0)),
        out_shape=jax.ShapeDtypeStruct((BATCH, n // BATCH, D_MODEL), F32),
        scratch_shapes=[pltpu.VMEM((D_MODEL // LANES, tm, LANES), F32)],
        compiler_params=_cparams("parallel"),
        name="final_norm",
    )(xs, g)


def _time_major_kernel(ctx_ref, x_ref, o_ref, slab_scr, *, n_ctx_tiles):
    steps = o_ref.shape[0] // BATCH
    n_slabs = slab_scr.shape[0]

    def emit(src_ref):
        for b in range(BATCH):
            for s in range(n_slabs):
                slab_scr[s, pl.ds(b, steps, stride=BATCH), :] = src_ref[b, :, s * LANES:(s + 1) * LANES]
        o_ref[...] = jnp.concatenate([slab_scr[s] for s in range(n_slabs)], axis=1)

    @pl.when(pl.program_id(0) < n_ctx_tiles)
    def _():
        emit(ctx_ref)

    @pl.when(pl.program_id(0) >= n_ctx_tiles)
    def _():
        emit(x_ref)


def _to_time_major(ctx, x):
    t_ctx, t_lat = ctx.shape[1], x.shape[1]
    tm = ROW_TILE
    steps = tm // BATCH
    n_ctx_tiles = t_ctx // steps
    n = (t_ctx + t_lat) * BATCH
    return pl.pallas_call(
        functools.partial(_time_major_kernel, n_ctx_tiles=n_ctx_tiles),
        grid=(n // tm,),
        in_specs=[
            pl.BlockSpec((BATCH, steps, D_MODEL), lambda i: (0, jnp.minimum(i, n_ctx_tiles - 1), 0)),
            pl.BlockSpec((BATCH, steps, D_MODEL), lambda i: (0, jnp.maximum(i - n_ctx_tiles, 0), 0)),
        ],
        out_specs=pl.BlockSpec((tm, D_MODEL), lambda i: (i, 0)),
        out_shape=jax.ShapeDtypeStruct((n, D_MODEL), F32),
        scratch_shapes=[pltpu.VMEM((D_MODEL // LANES, tm, LANES), F32)],
        compiler_params=_cparams("parallel"),
        name="to_time_major",
    )(ctx, x)


def _rotary_tables(t_ctx, t_lat):
    pos = jnp.arange(t_lat, dtype=jnp.int32)
    row = (pos // GRID_W).astype(F32)
    col = (pos % GRID_W).astype(F32)
    n_f = RET_DK // 4
    inv = ROPE_BASE ** (-jnp.arange(n_f, dtype=F32) / n_f)
    ang = jnp.concatenate([row[:, None] * inv, col[:, None] * inv], axis=-1)
    cos = jnp.concatenate([jnp.ones((t_ctx, RET_DK // 2), F32), jnp.cos(ang)], axis=0)
    sin = jnp.concatenate([jnp.zeros((t_ctx, RET_DK // 2), F32), jnp.sin(ang)], axis=0)
    cos128 = jnp.tile(cos, (1, 4))
    sin128 = jnp.tile(jnp.concatenate([-sin, sin], axis=1), (1, 2))
    return jnp.repeat(cos128, BATCH, axis=0), jnp.repeat(sin128, BATCH, axis=0)


def _lru_gate_weights(gate_w, gate_b):
    n_cc = D_MODEL // LRU_GW
    per = LRU_GW // LRU_BW
    gate_w = 0.5 * gate_w
    gate_b = 0.5 * gate_b
    w = gate_w.reshape(2, 2, n_cc, per, LRU_BW, LRU_BW)
    eye = jnp.eye(per, dtype=gate_w.dtype)
    bd = jnp.einsum('dgcpij,pq->dgcpiqj', w, eye).reshape(2, 2, n_cc, LRU_GW, LRU_GW)
    wcat = jnp.concatenate([bd[:, 0], bd[:, 1]], axis=-1).astype(BF16)
    b = gate_b.reshape(2, 2, n_cc, 1, LRU_GW)
    bcat = jnp.concatenate([b[:, 0], b[:, 1]], axis=-1)
    return wcat, bcat


def kernel(x, c, ctx, c_ctx, w_mod, b_mod, norm_g, ffn1_w_gu, ffn1_w_down, ffn2_w_gu, ffn2_w_down,
           w_in, ret_decay_logit, w_ret_o, lru_conv_w, lru_conv_b, lru_gate_w, lru_gate_b,
           lru_lambda, w_lru_o, w_out, final_g):
    depth = w_mod.shape[0]
    t_lat, t_ctx = x.shape[1], ctx.shape[1]
    t_all = t_lat + t_ctx
    n_rows = t_all * BATCH
    ctx_rows = t_ctx * BATCH
    assert x.shape[0] == BATCH and x.shape[2] == D_MODEL
    assert t_ctx % LRU_TB == 0 and t_lat % LRU_TB == 0 and ctx_rows % ROW_TILE == 0

    xs = _to_time_major(ctx, x)
    tabs = _mod_tables(c, c_ctx, w_mod, b_mod)
    cos_t, sin_t = _rotary_tables(t_ctx, t_lat)
    n_gw = D_MODEL // LRU_GW

    for l in range(depth):
        last = l == depth - 1
        tab = tabs[l]
        g = norm_g[l].reshape(3, 1, D_MODEL)
        xs = _ffn_sublayer(xs, tab, 0, g[0], ffn1_w_gu, ffn1_w_down, l, ctx_rows)
        q, k, v, z = _in_projection(xs, tab, g[1], cos_t, sin_t, w_in, l, ctx_rows)
        logit_rows = jnp.broadcast_to(ret_decay_logit[l].astype(F32)[:, :, None, None],
                                      (2, RET_HEADS, 1, LANES))
        o_seq = _retention(q, k, v, logit_rows, t_ctx // RET_CHUNK)
        gw, gb = _lru_gate_weights(lru_gate_w[l], lru_gate_b[l])
        hf, hb = _rg_lru(z, lru_conv_w[l], lru_conv_b[l].reshape(1, D_MODEL), gw, gb,
                         lru_lambda[l].reshape(2, n_gw, 1, LRU_GW), n_rows, t_ctx // LRU_TB)
        row_start = ctx_rows if last else 0
        xs = _merge(xs, tab, o_seq, hf, hb, z, w_ret_o[l].astype(BF16), w_lru_o[l].astype(BF16),
                    w_out[l].astype(BF16), ctx_rows, row_start)
        xs = _ffn_sublayer(xs, tab, 6, g[2], ffn2_w_gu, ffn2_w_down, l, ctx_rows - row_start)
    return _final_norm(xs, final_g.reshape(1, D_MODEL))
```

```python
import functools

import jax
import jax.numpy as jnp
from jax import lax
from jax.experimental import pallas as pl
from jax.experimental.pallas import tpu as pltpu

F32 = jnp.float32
BF16 = jnp.bfloat16

D_MODEL = 1024
BATCH = 8
LANES = 128
RET_HEADS = 8
RET_DK = 64
RET_DV = 128
RET_CHUNK = 128
HEAD_GROUP = 8
GRID_W = 64
ROPE_BASE = 10000.0
LRU_BLOCKS = 16
LRU_BW = D_MODEL // LRU_BLOCKS
LRU_C = 8.0
LRU_GW = 256
LRU_CW = 512
LRU_TB = 128
CONV_W = 4
FFN_HIDDEN = 2816
FFN_TF = 256
FFN_RES = 0.5
N_MOD = 9
EPS = 1e-6
PROJ_CHUNK = 512
N_PROJ_CHUNKS = 14
ROW_TILE = 512
VMEM_LIMIT = 52 * 1024 * 1024


def _cparams(*sem, fuse_inputs=None):
    return pltpu.CompilerParams(dimension_semantics=sem, vmem_limit_bytes=VMEM_LIMIT,
                                allow_input_fusion=fuse_inputs)


def _resident(shape):
    return pl.BlockSpec(shape, lambda *_: (0,) * len(shape), pipeline_mode=pl.Buffered(1))


def _sigmoid(x):
    return 1.0 / (1.0 + jnp.exp(-x))


def _ada_norm(x, g, shift, scale):
    rows = x.shape[0]
    ms = jnp.mean(x * x, axis=-1, keepdims=True)
    y = (x * lax.rsqrt(ms + EPS)) * g
    y3 = y.reshape(rows // BATCH, BATCH, D_MODEL)
    h = y3 * (1.0 + scale)[None] + shift[None]
    return h.reshape(rows, D_MODEL)


def _mod_kernel(c_ref, w_ref, b_ref, o_ref):
    c = c_ref[...]
    s = (c * _sigmoid(c)).astype(BF16)
    o_ref[...] = jnp.dot(s, w_ref[...].astype(BF16), preferred_element_type=F32) + b_ref[...]


def _mod_tables(c, c_ctx, w_mod, b_mod):
    depth = w_mod.shape[0]
    cc = jnp.zeros((2 * BATCH, D_MODEL), F32).at[:BATCH].set(c).at[BATCH].set(c_ctx)
    out = pl.pallas_call(
        _mod_kernel,
        grid=(depth, N_MOD),
        in_specs=[
            pl.BlockSpec((2 * BATCH, D_MODEL), lambda l, j: (0, 0)),
            pl.BlockSpec((None, D_MODEL, D_MODEL), lambda l, j: (l, 0, j)),
            pl.BlockSpec((None, 1, D_MODEL), lambda l, j: (l, 0, j)),
        ],
        out_specs=pl.BlockSpec((None, 2 * BATCH, D_MODEL), lambda l, j: (l, 0, j)),
        out_shape=jax.ShapeDtypeStruct((depth, 2 * BATCH, N_MOD * D_MODEL), F32),
        compiler_params=_cparams("parallel", "parallel"),
        name="adaln_mod",
    )(cc, w_mod, b_mod.reshape(depth, 1, N_MOD * D_MODEL))
    out = out.reshape(depth, 2 * BATCH, N_MOD, D_MODEL)
    lat = out[:, :BATCH].transpose(0, 2, 1, 3)
    ctx = jnp.broadcast_to(out[:, BATCH][:, :, None, :], lat.shape)
    return jnp.stack([ctx, lat], axis=1)


def _ffn_kernel(x_ref, mod_ref, g_ref, wg32_ref, wu32_ref, wd32_ref, o_ref,
                wg_s, wu_s, wd_s, h_scr, act_scr, acc_scr):
    nf = FFN_HIDDEN // FFN_TF
    step = pl.program_id(0)

    @pl.when(step < nf)
    def _():
        wg_s[step] = wg32_ref[...].astype(BF16)
        wu_s[step] = wu32_ref[...].astype(BF16)
        wd_s[step] = wd32_ref[...].astype(BF16)

    @pl.when(step >= nf)
    def _():
        rows = x_ref.shape[0]
        x = x_ref[...]
        h_scr[...] = _ada_norm(x, g_ref[...], mod_ref[0], mod_ref[1]).astype(BF16)
        acc_scr[...] = jnp.zeros_like(acc_scr)

        def gated(k):
            hb = h_scr[...]
            u = jnp.dot(hb, wg_s[k], preferred_element_type=F32)
            v = jnp.dot(hb, wu_s[k], preferred_element_type=F32)
            return ((u * _sigmoid(u)) * v).astype(BF16)

        def down(k):
            return jnp.dot(act_scr[...], wd_s[k], preferred_element_type=F32)

        act_scr[...] = gated(0)
        for k in range(1, nf):
            part = down(k - 1)
            act_scr[...] = gated(k)
            acc_scr[...] += part
        acc_scr[...] += down(nf - 1)
        y3 = acc_scr[...].reshape(rows // BATCH, BATCH, D_MODEL)
        x3 = x.reshape(rows // BATCH, BATCH, D_MODEL)
        o_ref[...] = (x3 + (FFN_RES * mod_ref[2])[None] * y3).reshape(rows, D_MODEL)


def _ffn_sublayer(xs, tab, sub, g, w_gu, w_down, layer, ctx_rows):
    n = xs.shape[0]
    tm = ROW_TILE
    nf = FFN_HIDDEN // FFN_TF
    tile = lambda i: jnp.maximum(i - nf, 0)
    chunk = lambda i: jnp.minimum(i, nf - 1)
    return pl.pallas_call(
        _ffn_kernel,
        grid=(nf + n // tm,),
        in_specs=[
            pl.BlockSpec((tm, D_MODEL), lambda i: (tile(i), 0)),
            pl.BlockSpec((None, 3, BATCH, D_MODEL),
                         lambda i: ((tile(i) * tm >= ctx_rows).astype(jnp.int32), sub // 3, 0, 0)),
            _resident((1, D_MODEL)),
            pl.BlockSpec((None, D_MODEL, FFN_TF), lambda i: (layer, 0, chunk(i))),
            pl.BlockSpec((None, D_MODEL, FFN_TF), lambda i: (layer, 0, nf + chunk(i))),
            pl.BlockSpec((None, FFN_TF, D_MODEL), lambda i: (layer, chunk(i), 0)),
        ],
        out_specs=pl.BlockSpec((tm, D_MODEL), lambda i: (tile(i), 0)),
        out_shape=jax.ShapeDtypeStruct((n, D_MODEL), F32),
        scratch_shapes=[pltpu.VMEM((nf, D_MODEL, FFN_TF), BF16), pltpu.VMEM((nf, D_MODEL, FFN_TF), BF16),
                        pltpu.VMEM((nf, FFN_TF, D_MODEL), BF16),
                        pltpu.VMEM((tm, D_MODEL), BF16), pltpu.VMEM((tm, FFN_TF), BF16),
                        pltpu.VMEM((tm, D_MODEL), F32)],
        compiler_params=_cparams("arbitrary"),
        name="ffn_sublayer",
    )(xs, tab, g, w_gu, w_gu, w_down)


def _swap_halves(a):
    w = a.shape[1]
    lane = lax.broadcasted_iota(jnp.int32, a.shape, 1)
    first_half = (lane % RET_DK) < (RET_DK // 2)
    return jnp.where(first_half, pltpu.roll(a, w - RET_DK // 2, 1), pltpu.roll(a, RET_DK // 2, 1))


def _rows_to_sequences(a, slab_scr, out_ref):
    rows, width = a.shape
    steps = rows // BATCH
    n_slabs = width // LANES
    for s in range(n_slabs):
        slab_scr[s] = a[:, s * LANES:(s + 1) * LANES]
    for b in range(BATCH):
        for s in range(n_slabs):
            col = b * width + s * LANES
            out_ref[:, col:col + LANES] = slab_scr[s, pl.ds(b, steps, stride=BATCH), :].astype(out_ref.dtype)


def _inproj_kernel(x_ref, mod_ref, g_ref, cos_ref, sin_ref, w32_ref, q_ref, k_ref, v_ref, z_ref,
                   w_s, h_scr, slab_scr):
    step = pl.program_id(0)

    @pl.when(step < N_PROJ_CHUNKS)
    def _():
        w_s[step] = w32_ref[...].astype(BF16)

    @pl.when(step >= N_PROJ_CHUNKS)
    def _():
        h_scr[...] = _ada_norm(x_ref[...], g_ref[...], mod_ref[0], mod_ref[1]).astype(BF16)
        reps = PROJ_CHUNK // cos_ref.shape[1]
        cos = jnp.tile(cos_ref[...], (1, reps))
        sin = jnp.tile(sin_ref[...], (1, reps))

        def rotate(a):
            return a * cos + _swap_halves(a) * sin

        def proj(c):
            return jnp.dot(h_scr[...], w_s[c], preferred_element_type=F32)

        _rows_to_sequences(rotate(proj(0)), slab_scr, q_ref)
        _rows_to_sequences(rotate(proj(1)) * (RET_DK ** -0.5), slab_scr, k_ref)
        for c in range(2):
            _rows_to_sequences(proj(2 + c), slab_scr, v_ref.at[c])
        for c in range(4, N_PROJ_CHUNKS):
            z_ref[c - 4] = proj(c).astype(BF16)


def _in_projection(xs, tab, g, cos_t, sin_t, w_in, layer, ctx_rows):
    n = xs.shape[0]
    tm = ROW_TILE
    nz = N_PROJ_CHUNKS - 4
    steps = tm // BATCH
    t_all = n // BATCH
    seq_w = BATCH * PROJ_CHUNK
    nw = N_PROJ_CHUNKS
    tile = lambda i: jnp.maximum(i - nw, 0)
    return pl.pallas_call(
        _inproj_kernel,
        grid=(nw + n // tm,),
        in_specs=[
            pl.BlockSpec((tm, D_MODEL), lambda i: (tile(i), 0)),
            pl.BlockSpec((None, 3, BATCH, D_MODEL),
                         lambda i: ((tile(i) * tm >= ctx_rows).astype(jnp.int32), 1, 0, 0)),
            _resident((1, D_MODEL)),
            pl.BlockSpec((tm, LANES), lambda i: (tile(i), 0)),
            pl.BlockSpec((tm, LANES), lambda i: (tile(i), 0)),
            pl.BlockSpec((None, D_MODEL, PROJ_CHUNK), lambda i: (layer, 0, jnp.minimum(i, nw - 1))),
        ],
        out_specs=[
            pl.BlockSpec((steps, seq_w), lambda i: (tile(i), 0)),
            pl.BlockSpec((steps, seq_w), lambda i: (tile(i), 0)),
            pl.BlockSpec((2, steps, seq_w), lambda i: (0, tile(i), 0)),
            pl.BlockSpec((nz, tm, PROJ_CHUNK), lambda i: (0, tile(i), 0)),
        ],
        out_shape=[
            jax.ShapeDtypeStruct((t_all, seq_w), BF16),
            jax.ShapeDtypeStruct((t_all, seq_w), BF16),
            jax.ShapeDtypeStruct((2, t_all, seq_w), BF16),
            jax.ShapeDtypeStruct((nz, n, PROJ_CHUNK), BF16),
        ],
        scratch_shapes=[pltpu.VMEM((N_PROJ_CHUNKS, D_MODEL, PROJ_CHUNK), BF16),
                        pltpu.VMEM((tm, D_MODEL), BF16),
                        pltpu.VMEM((PROJ_CHUNK // LANES, tm, LANES), F32)],
        compiler_params=_cparams("arbitrary"),
        name="in_projection",
    )(xs, tab, g, cos_t, sin_t, w_in)


def _log_sigmoid(x):
    return jnp.minimum(x, 0.0) - jnp.log1p(jnp.exp(-jnp.abs(x)))


def _dot_t1(a, b):
    return lax.dot_general(a, b, (((1,), (1,)), ((), ())), preferred_element_type=F32)


def _retention_kernel(lg_ref, q_ref, k_ref, v_ref, o_ref,
                      dmat, qdec, kdec, cdec, sb_scr, sf_scr, *, n_ctx_chunks):
    c_len = RET_CHUNK
    pair_k = 2 * RET_DK
    pair_v = 2 * RET_DV
    n_pairs = HEAD_GROUP // 2
    n_chunks = q_ref.shape[0] // c_len
    hg = pl.program_id(1)
    row_i = lax.broadcasted_iota(jnp.int32, (c_len, c_len), 0)
    col_i = lax.broadcasted_iota(jnp.int32, (c_len, c_len), 1)
    row = row_i.astype(F32)
    col = col_i.astype(F32)
    rel = row - col
    first_lanes = col_i < RET_DK
    first_rows = row_i < RET_DK
    own_block = (lax.broadcasted_iota(jnp.int32, (pair_k, pair_v), 0) < RET_DK) == (
        lax.broadcasted_iota(jnp.int32, (pair_k, pair_v), 1) < RET_DV)

    for p in range(n_pairs):
        h0 = hg * HEAD_GROUP + 2 * p
        lg = [[_log_sigmoid(lg_ref[d, h0 + e]) for e in range(2)] for d in range(2)]
        for e in range(2):
            dmat[2 * p + e] = jnp.where(rel >= 0.0, jnp.exp(lg[0][e] * jnp.maximum(rel, 0.0)),
                                        jnp.exp(lg[1][e] * jnp.maximum(-rel, 0.0)))
        lane_f = jnp.where(first_lanes, lg[0][0], lg[0][1])
        lane_b = jnp.where(first_lanes, lg[1][0], lg[1][1])
        row_f = jnp.where(first_rows, lg[0][0], lg[0][1])
        row_b = jnp.where(first_rows, lg[1][0], lg[1][1])
        qdec[0, p] = jnp.exp(lane_f * (row + 1.0))
        qdec[1, p] = jnp.exp(lane_b * (c_len - row))
        kdec[0, p] = jnp.exp(row_f * (c_len - 1.0 - col))
        kdec[1, p] = jnp.exp(row_b * col)
        cdec[0, p] = jnp.exp(row_f * c_len)
        cdec[1, p] = jnp.exp(row_b * c_len)

    def advance(d, p, s, kp, vp):
        kd = (kp.astype(F32).T * kdec[d, p]).astype(BF16)
        upd = jnp.dot(kd, vp, preferred_element_type=F32)
        cd = cdec[d, p]
        return s * jnp.concatenate([cd, cd], axis=1) + jnp.where(own_block, upd, 0.0)

    sf_scr[...] = jnp.zeros_like(sf_scr)

    def bwd_body(i, carry):
        c = jnp.where(i < n_ctx_chunks, n_ctx_chunks - 1 - i, n_chunks - 1 + n_ctx_chunks - i)
        rows = pl.ds(pl.multiple_of(c * c_len, c_len), c_len)
        for p in range(n_pairs):
            kp = k_ref[rows, p * pair_k:(p + 1) * pair_k]
            vp = v_ref[(p * pair_v) // PROJ_CHUNK, rows, pl.ds((p * pair_v) % PROJ_CHUNK, pair_v)]
            s = sf_scr[p]
            sb_scr[c, p] = s.astype(BF16)
            sf_scr[p] = advance(1, p, s, kp, vp)
        return carry

    lax.fori_loop(0, n_chunks, bwd_body, 0, unroll=3)

    sf_scr[...] = jnp.zeros_like(sf_scr)
    zero_v = jnp.zeros((c_len, RET_DV), BF16)
    zero_q = jnp.zeros((c_len, pair_k), BF16)

    def fwd_body(c, carry):
        rows = pl.ds(pl.multiple_of(c * c_len, c_len), c_len)
        for p in range(n_pairs):
            qp = q_ref[rows, p * pair_k:(p + 1) * pair_k]
            kp = k_ref[rows, p * pair_k:(p + 1) * pair_k]
            vp = v_ref[(p * pair_v) // PROJ_CHUNK, rows, pl.ds((p * pair_v) % PROJ_CHUNK, pair_v)]
            q_split = jnp.concatenate([jnp.where(first_lanes, qp, zero_q),
                                       jnp.where(first_lanes, zero_q, qp)], axis=0)
            scores = _dot_t1(q_split, kp)
            pa = (scores[:c_len] * dmat[2 * p]).astype(BF16)
            pb = (scores[c_len:] * dmat[2 * p + 1]).astype(BF16)
            qf = qp.astype(F32)
            s = sf_scr[p]
            lhs = jnp.concatenate([pa, pb, (qf * qdec[0, p]).astype(BF16), (qf * qdec[1, p]).astype(BF16)],
                                  axis=1)
            rhs = jnp.concatenate([
                jnp.concatenate([vp[:, :RET_DV], zero_v], axis=1),
                jnp.concatenate([zero_v, vp[:, RET_DV:]], axis=1),
                s.astype(BF16), sb_scr[c, p]], axis=0)
            o = jnp.dot(lhs, rhs, preferred_element_type=F32)
            sf_scr[p] = advance(0, p, s, kp, vp)
            for e in range(2):
                oe = o[:, e * RET_DV:(e + 1) * RET_DV]
                mu = jnp.mean(oe, axis=-1, keepdims=True)
                d = oe - mu
                var = jnp.mean(d * d, axis=-1, keepdims=True)
                col0 = p * pair_v + e * RET_DV
                o_ref[rows, col0:col0 + RET_DV] = (d * lax.rsqrt(var + EPS)).astype(BF16)
        return carry

    lax.fori_loop(0, n_chunks, fwd_body, 0, unroll=2)


def _retention(q, k, v, logit_rows, n_ctx_chunks):
    t_all = q.shape[0]
    n_groups = RET_HEADS // HEAD_GROUP
    n_pairs = HEAD_GROUP // 2
    qw = HEAD_GROUP * RET_DK
    vw = HEAD_GROUP * RET_DV
    v_chunks = vw // PROJ_CHUNK
    assert vw % PROJ_CHUNK == 0 and n_groups * v_chunks == v.shape[0]
    n_chunks = t_all // RET_CHUNK
    return pl.pallas_call(
        functools.partial(_retention_kernel, n_ctx_chunks=n_ctx_chunks),
        grid=(BATCH, n_groups),
        in_specs=[
            _resident((2, RET_HEADS, 1, LANES)),
            pl.BlockSpec((t_all, qw), lambda b, h: (0, b * n_groups + h)),
            pl.BlockSpec((t_all, qw), lambda b, h: (0, b * n_groups + h)),
            pl.BlockSpec((v_chunks, t_all, PROJ_CHUNK), lambda b, h: (h, 0, b)),
        ],
        out_specs=pl.BlockSpec((t_all, vw), lambda b, h: (0, b * n_groups + h)),
        out_shape=jax.ShapeDtypeStruct((t_all, BATCH * RET_HEADS * RET_DV), BF16),
        scratch_shapes=[
            pltpu.VMEM((HEAD_GROUP, RET_CHUNK, RET_CHUNK), F32),
            pltpu.VMEM((2, n_pairs, RET_CHUNK, 2 * RET_DK), F32),
            pltpu.VMEM((2, n_pairs, 2 * RET_DK, RET_CHUNK), F32),
            pltpu.VMEM((2, n_pairs, 2 * RET_DK, RET_CHUNK), F32),
            pltpu.VMEM((n_chunks, n_pairs, 2 * RET_DK, 2 * RET_DV), BF16),
            pltpu.VMEM((n_pairs, 2 * RET_DK, 2 * RET_DV), F32),
        ],
        compiler_params=_cparams("parallel", "parallel"),
        name="retention",
    )(logit_rows, q, k, v)


def _lru_kernel(xf_ref, xfp_ref, xfn_ref, xb_ref, xbp_ref, xbn_ref,
                cw_ref, cb_ref, gw_ref, gb_ref, lam_ref, hf_ref, hb_ref,
                a_scr, b_scr, h_scr, *, n_ctx_blocks, n_blocks):
    j = pl.program_id(1)
    jb = jnp.where(j < n_ctx_blocks, n_ctx_blocks - 1 - j, n_blocks - 1 + n_ctx_blocks - j)
    rows = xf_ref.shape[0]
    halo = xfp_ref.shape[0]
    cw = cw_ref[...]
    cb = cb_ref[...]

    def prepare(d, blk, x_ref, xp_ref, xn_ref):
        is_first = jnp.logical_or(blk == 0, blk == n_ctx_blocks)
        is_last = jnp.logical_or(blk == n_ctx_blocks - 1, blk == n_blocks - 1)
        prev = jnp.where(is_first, 0.0, xp_ref[...].astype(F32))
        nxt = jnp.where(is_last, 0.0, xn_ref[...].astype(F32))
        xe = jnp.concatenate([prev, x_ref[...].astype(F32), nxt], axis=0)
        u = cb
        for tap in range(CONV_W):
            off = halo - (2 - tap) * BATCH
            u = u + xe[off:off + rows] * cw[tap:tap + 1]
        for grp in range(LRU_CW // LRU_GW):
            cols = slice(grp * LRU_GW, (grp + 1) * LRU_GW)
            ug = u[:, cols]
            t = jnp.tanh(jnp.dot(ug.astype(BF16), gw_ref[d, grp], preferred_element_type=F32)
                         + gb_ref[d, grp])
            lam = lam_ref[d, grp]
            softplus = jnp.maximum(-lam, 0.0) + jnp.log1p(jnp.exp(-jnp.abs(lam)))
            half_c = (-0.5 * LRU_C) * softplus
            log_a = half_c * t[:, :LRU_GW] + half_c
            half_u = 0.5 * ug
            iu = half_u * t[:, LRU_GW:] + half_u
            th = jnp.tanh(log_a)
            w = (-2.0 * th) / (1.0 - th)
            a_scr[d, :, cols] = jnp.exp(log_a)
            b_scr[d, :, cols] = jnp.where(w > 0.0, w * lax.rsqrt(w), 0.0) * iu

    prepare(0, j, xf_ref, xfp_ref, xfn_ref)
    prepare(1, jb, xb_ref, xbp_ref, xbn_ref)

    @pl.when(j == 0)
    def _():
        h_scr[...] = jnp.zeros_like(h_scr)

    n_steps = rows // BATCH

    def step(s, carry):
        hf, hb = carry
        rf = pl.ds(pl.multiple_of(s * BATCH, BATCH), BATCH)
        hf = a_scr[0, rf, :] * hf + b_scr[0, rf, :]
        hf_ref[rf, :] = hf
        rb = pl.ds(pl.multiple_of((n_steps - 1 - s) * BATCH, BATCH), BATCH)
        hb = a_scr[1, rb, :] * hb + b_scr[1, rb, :]
        hb_ref[rb, :] = hb
        return hf, hb

    hf, hb = lax.fori_loop(0, n_steps, step, (h_scr[0], h_scr[1]), unroll=8)
    h_scr[0] = hf
    h_scr[1] = hb


def _rg_lru(z, conv_w, conv_b, gate_w, gate_b, lam, n_rows, n_ctx_blocks):
    rows = LRU_TB * BATCH
    n_blocks = n_rows // rows
    n_cc = D_MODEL // LRU_CW
    n_grp = LRU_CW // LRU_GW
    per_chunk = PROJ_CHUNK // LRU_CW
    halo = 16
    hb_per_block = rows // halo
    n_halo = n_rows // halo

    def bwd_block(j):
        return jnp.where(j < n_ctx_blocks, n_ctx_blocks - 1 - j, n_blocks - 1 + n_ctx_blocks - j)

    def cur(f):
        return pl.BlockSpec((None, rows, LRU_CW), lambda c, j: (2 + c // per_chunk, f(j), c % per_chunk))

    def prev(f):
        return pl.BlockSpec((None, halo, LRU_CW),
                            lambda c, j: (2 + c // per_chunk, jnp.maximum(f(j) * hb_per_block - 1, 0),
                                          c % per_chunk))

    def nxt(f):
        return pl.BlockSpec((None, halo, LRU_CW),
                            lambda c, j: (2 + c // per_chunk,
                                          jnp.minimum((f(j) + 1) * hb_per_block, n_halo - 1),
                                          c % per_chunk))

    ident = lambda j: j
    return pl.pallas_call(
        functools.partial(_lru_kernel, n_ctx_blocks=n_ctx_blocks, n_blocks=n_blocks),
        grid=(n_cc, n_blocks),
        in_specs=[
            cur(ident), prev(ident), nxt(ident), cur(bwd_block), prev(bwd_block), nxt(bwd_block),
            pl.BlockSpec((CONV_W, LRU_CW), lambda c, j: (0, c)),
            pl.BlockSpec((1, LRU_CW), lambda c, j: (0, c)),
            pl.BlockSpec((2, n_grp, LRU_GW, 2 * LRU_GW), lambda c, j: (0, c, 0, 0)),
            pl.BlockSpec((2, n_grp, 1, 2 * LRU_GW), lambda c, j: (0, c, 0, 0)),
            pl.BlockSpec((2, n_grp, 1, LRU_GW), lambda c, j: (0, c, 0, 0)),
        ],
        out_specs=[
            pl.BlockSpec((rows, LRU_CW), lambda c, j: (j, c)),
            pl.BlockSpec((rows, LRU_CW), lambda c, j: (bwd_block(j), c)),
        ],
        out_shape=[jax.ShapeDtypeStruct((n_rows, D_MODEL), F32)] * 2,
        scratch_shapes=[
            pltpu.VMEM((2, rows, LRU_CW), F32),
            pltpu.VMEM((2, rows, LRU_CW), F32),
            pltpu.VMEM((2, BATCH, LRU_CW), F32),
        ],
        compiler_params=_cparams("parallel", "arbitrary"),
        name="rg_lru",
    )(z, z, z, z, z, z, conv_w, conv_b, gate_w, gate_b, lam)


def _gelu_tanh(x):
    return 0.5 * x * (1.0 + jnp.tanh(0.7978845608028654 * (x + 0.044715 * (x * x * x))))


def _sequences_to_rows(o_ref, slab_scr):
    steps = o_ref.shape[0]
    n_slabs = slab_scr.shape[0]
    width = n_slabs * LANES
    for b in range(BATCH):
        for s in range(n_slabs):
            col = b * width + s * LANES
            slab_scr[s, pl.ds(b, steps, stride=BATCH), :] = o_ref[:, col:col + LANES].astype(F32)


def _merge_kernel(x_ref, mod_ref, o_ref_in, hf_ref, hb_ref, gr0_ref, gr1_ref, gl0_ref, gl1_ref,
                  ga0_ref, ga1_ref, gb0_ref, gb1_ref, wr_ref, wl_ref, wo_ref, o_ref, slab_scr):
    rows = x_ref.shape[0]
    n_slabs = slab_scr.shape[0]
    _sequences_to_rows(o_ref_in, slab_scr)

    def both(r0, r1):
        return jnp.concatenate([r0[...], r1[...]], axis=1).astype(F32)

    gr = both(gr0_ref, gr1_ref)
    o_ret = jnp.concatenate([slab_scr[s] for s in range(n_slabs)], axis=1) * (gr * _sigmoid(gr))
    y_a = jnp.dot(o_ret.astype(BF16), wr_ref[...], preferred_element_type=F32)
    h = hf_ref[...] + hb_ref[...]
    y_b = jnp.dot((h * _gelu_tanh(both(gl0_ref, gl1_ref))).astype(BF16), wl_ref[...],
                  preferred_element_type=F32)
    m = (_sigmoid(both(ga0_ref, ga1_ref)) * y_a + _sigmoid(both(gb0_ref, gb1_ref)) * y_b).astype(BF16)
    y = jnp.dot(m, wo_ref[...], preferred_element_type=F32)
    y3 = y.reshape(rows // BATCH, BATCH, D_MODEL)
    x3 = x_ref[...].reshape(rows // BATCH, BATCH, D_MODEL)
    o_ref[...] = (x3 + mod_ref[2][None] * y3).reshape(rows, D_MODEL)


def _merge(xs, tab, o_seq, hf, hb, z, w_ret_o, w_lru_o, w_out, ctx_rows, row_start):
    n = xs.shape[0]
    tm = ROW_TILE
    t0 = row_start // tm

    def zc(c):
        return pl.BlockSpec((None, tm, PROJ_CHUNK), lambda i: (c, i + t0, 0))

    row = lambda w: pl.BlockSpec((tm, w), lambda i: (i + t0, 0))
    return pl.pallas_call(
        _merge_kernel,
        grid=(n // tm - t0,),
        in_specs=[
            row(D_MODEL),
            pl.BlockSpec((None, 3, BATCH, D_MODEL),
                         lambda i: (((i + t0) * tm >= ctx_rows).astype(jnp.int32), 1, 0, 0)),
            pl.BlockSpec((tm // BATCH, BATCH * D_MODEL), lambda i: (i + t0, 0)),
            row(D_MODEL), row(D_MODEL),
            zc(0), zc(1), zc(4), zc(5), zc(6), zc(7), zc(8), zc(9),
            _resident((D_MODEL, D_MODEL)), _resident((D_MODEL, D_MODEL)), _resident((D_MODEL, D_MODEL)),
        ],
        out_specs=pl.BlockSpec((tm, D_MODEL), lambda i: (i, 0)),
        out_shape=jax.ShapeDtypeStruct((n - row_start, D_MODEL), F32),
        scratch_shapes=[pltpu.VMEM((D_MODEL // LANES, tm, LANES), F32)],
        compiler_params=_cparams("parallel", fuse_inputs=[False] * 13 + [True] * 3),
        name="merge_out_proj",
    )(xs, tab, o_seq, hf, hb, z, z, z, z, z, z, z, z, w_ret_o, w_lru_o, w_out)


def _final_kernel(x_ref, g_ref, o_ref, slab_scr):
    x = x_ref[...]
    steps = x.shape[0] // BATCH
    ms = jnp.mean(x * x, axis=-1, keepdims=True)
    y = (x * lax.rsqrt(ms + EPS)) * g_ref[...]
    n_slabs = slab_scr.shape[0]
    for s in range(n_slabs):
        slab_scr[s] = y[:, s * LANES:(s + 1) * LANES]
    for b in range(BATCH):
        for s in range(n_slabs):
            o_ref[b, :, s * LANES:(s + 1) * LANES] = slab_scr[s, pl.ds(b, steps, stride=BATCH), :]


def _final_norm(xs, g):
    n = xs.shape[0]
    tm = ROW_TILE
    return pl.pallas_call(
        _final_kernel,
        grid=(n // tm,),
        in_specs=[
            pl.BlockSpec((tm, D_MODEL), lambda i: (i, 0)),
            pl.BlockSpec((1, D_MODEL), lambda i: (0, 0)),
        ],
        out_specs=pl.BlockSpec((BATCH, tm // BATCH, D_MODEL), lambda i: (0, i, 0)),
        out_shape=jax.ShapeDtypeStruct((BATCH, n // BATCH, D_MODEL), F32),
        scratch_shapes=[pltpu.VMEM((D_MODEL // LANES, tm, LANES), F32)],
        compiler_params=_cparams("parallel"),
        name="final_norm",
    )(xs, g)


def _time_major_kernel(ctx_ref, x_ref, o_ref, slab_scr, *, n_ctx_tiles):
    steps = o_ref.shape[0] // BATCH
    n_slabs = slab_scr.shape[0]

    def emit(src_ref):
        for b in range(BATCH):
            for s in range(n_slabs):
                slab_scr[s, pl.ds(b, steps, stride=BATCH), :] = src_ref[b, :, s * LANES:(s + 1) * LANES]
        o_ref[...] = jnp.concatenate([slab_scr[s] for s in range(n_slabs)], axis=1)

    @pl.when(pl.program_id(0) < n_ctx_tiles)
    def _():
        emit(ctx_ref)

    @pl.when(pl.program_id(0) >= n_ctx_tiles)
    def _():
        emit(x_ref)


def _to_time_major(ctx, x):
    t_ctx, t_lat = ctx.shape[1], x.shape[1]
    tm = ROW_TILE
    steps = tm // BATCH
    n_ctx_tiles = t_ctx // steps
    n = (t_ctx + t_lat) * BATCH
    return pl.pallas_call(
        functools.partial(_time_major_kernel, n_ctx_tiles=n_ctx_tiles),
        grid=(n // tm,),
        in_specs=[
            pl.BlockSpec((BATCH, steps, D_MODEL), lambda i: (0, jnp.minimum(i, n_ctx_tiles - 1), 0)),
            pl.BlockSpec((BATCH, steps, D_MODEL), lambda i: (0, jnp.maximum(i - n_ctx_tiles, 0), 0)),
        ],
        out_specs=pl.BlockSpec((tm, D_MODEL), lambda i: (i, 0)),
        out_shape=jax.ShapeDtypeStruct((n, D_MODEL), F32),
        scratch_shapes=[pltpu.VMEM((D_MODEL // LANES, tm, LANES), F32)],
        compiler_params=_cparams("parallel"),
        name="to_time_major",
    )(ctx, x)


def _rotary_tables(t_ctx, t_lat):
    pos = jnp.arange(t_lat, dtype=jnp.int32)
    row = (pos // GRID_W).astype(F32)
    col = (pos % GRID_W).astype(F32)
    n_f = RET_DK // 4
    inv = ROPE_BASE ** (-jnp.arange(n_f, dtype=F32) / n_f)
    ang = jnp.concatenate([row[:, None] * inv, col[:, None] * inv], axis=-1)
    cos = jnp.concatenate([jnp.ones((t_ctx, RET_DK // 2), F32), jnp.cos(ang)], axis=0)
    sin = jnp.concatenate([jnp.zeros((t_ctx, RET_DK // 2), F32), jnp.sin(ang)], axis=0)
    cos128 = jnp.tile(cos, (1, 4))
    sin128 = jnp.tile(jnp.concatenate([-sin, sin], axis=1), (1, 2))
    return jnp.repeat(cos128, BATCH, axis=0), jnp.repeat(sin128, BATCH, axis=0)


def _lru_gate_weights(gate_w, gate_b):
    n_cc = D_MODEL // LRU_GW
    per = LRU_GW // LRU_BW
    gate_w = 0.5 * gate_w
    gate_b = 0.5 * gate_b
    w = gate_w.reshape(2, 2, n_cc, per, LRU_BW, LRU_BW)
    eye = jnp.eye(per, dtype=gate_w.dtype)
    bd = jnp.einsum('dgcpij,pq->dgcpiqj', w, eye).reshape(2, 2, n_cc, LRU_GW, LRU_GW)
    wcat = jnp.concatenate([bd[:, 0], bd[:, 1]], axis=-1).astype(BF16)
    b = gate_b.reshape(2, 2, n_cc, 1, LRU_GW)
    bcat = jnp.concatenate([b[:, 0], b[:, 1]], axis=-1)
    return wcat, bcat


def kernel(x, c, ctx, c_ctx, w_mod, b_mod, norm_g, ffn1_w_gu, ffn1_w_down, ffn2_w_gu, ffn2_w_down,
           w_in, ret_decay_logit, w_ret_o, lru_conv_w, lru_conv_b, lru_gate_w, lru_gate_b,
           lru_lambda, w_lru_o, w_out, final_g):
    depth = w_mod.shape[0]
    t_lat, t_ctx = x.shape[1], ctx.shape[1]
    t_all = t_lat + t_ctx
    n_rows = t_all * BATCH
    ctx_rows = t_ctx * BATCH
    assert x.shape[0] == BATCH and x.shape[2] == D_MODEL
    assert t_ctx % LRU_TB == 0 and t_lat % LRU_TB == 0 and ctx_rows % ROW_TILE == 0

    xs = _to_time_major(ctx, x)
    tabs = _mod_tables(c, c_ctx, w_mod, b_mod)
    cos_t, sin_t = _rotary_tables(t_ctx, t_lat)
    n_gw = D_MODEL // LRU_GW

    for l in range(depth):
        last = l == depth - 1
        tab = tabs[l]
        g = norm_g[l].reshape(3, 1, D_MODEL)
        xs = _ffn_sublayer(xs, tab, 0, g[0], ffn1_w_gu, ffn1_w_down, l, ctx_rows)
        q, k, v, z = _in_projection(xs, tab, g[1], cos_t, sin_t, w_in, l, ctx_rows)
        logit_rows = jnp.broadcast_to(ret_decay_logit[l].astype(F32)[:, :, None, None],
                                      (2, RET_HEADS, 1, LANES))
        o_seq = _retention(q, k, v, logit_rows, t_ctx // RET_CHUNK)
        gw, gb = _lru_gate_weights(lru_gate_w[l], lru_gate_b[l])
        hf, hb = _rg_lru(z, lru_conv_w[l], lru_conv_b[l].reshape(1, D_MODEL), gw, gb,
                         lru_lambda[l].reshape(2, n_gw, 1, LRU_GW), n_rows, t_ctx // LRU_TB)
        row_start = ctx_rows if last else 0
        xs = _merge(xs, tab, o_seq, hf, hb, z, w_ret_o[l].astype(BF16), w_lru_o[l].astype(BF16),
                    w_out[l].astype(BF16), ctx_rows, row_start)
        xs = _ffn_sublayer(xs, tab, 6, g[2], ffn2_w_gu, ffn2_w_down, l, ctx_rows - row_start)
    return _final_norm(xs, final_g.reshape(1, D_MODEL))
```
